```python
import jax
import jax.numpy as jnp
from jax import lax
import numpy as np

D_MODEL = 2048
BATCH = 8
SEQ = 2048
DEPTH = 4

GRID_W = 64
CTX_LEN = 256
EPS = 1e-6
ROPE_THETA = 10000.0
ROPE_DIM = 64

MLA_HEADS = 4
MLA_NOPE = 128
MLA_ROPE = ROPE_DIM
MLA_V = 128
MLA_Q_RANK = 512
MLA_KV_RANK = 512
MLA_QK = MLA_NOPE + MLA_ROPE
Q_BLOCK = 128

GQA_HEADS = 8
GQA_KV_HEADS = 2
GQA_GROUP = GQA_HEADS // GQA_KV_HEADS
GQA_HD = ROPE_DIM
WINDOW = 128
BLOCK = 128

CONV_CH = 512
CONV_K = 31

POOL_WINDOWS = (2, 4, 8, 16)
POOL_GROUPS = 4
POOL_GC = 128
POOL_CH = POOL_GROUPS * POOL_GC

N_BRANCH = 4
BRANCH_W = 512

D_FF = 5504
FFN_CONV_K = 3

N_MOD = 6

KV_SPLITS = (MLA_KV_RANK, MLA_KV_RANK + MLA_ROPE, MLA_KV_RANK + MLA_ROPE + GQA_KV_HEADS * GQA_HD)
KV_COLS = MLA_KV_RANK + MLA_ROPE + 2 * GQA_KV_HEADS * GQA_HD
Q_COLS = MLA_Q_RANK + GQA_HEADS * GQA_HD
CONV_COLS = 2 * CONV_CH
GATE_COLS = N_BRANCH * D_MODEL
IN_SPLITS = (KV_COLS, KV_COLS + Q_COLS, KV_COLS + Q_COLS + CONV_COLS, KV_COLS + Q_COLS + CONV_COLS + POOL_CH)
IN_COLS = KV_COLS + Q_COLS + CONV_COLS + POOL_CH + GATE_COLS

kernel_name = 'hybrid_dit_prefix_ctx_gated_parallel_mixers'


def _rmsnorm(x, g):
    xf = x.astype(jnp.float32)
    y = xf * lax.rsqrt(jnp.mean(xf * xf, axis=-1, keepdims=True) + EPS)
    return (y * g.astype(jnp.float32)).astype(x.dtype)


def _layernorm(x, g, b):
    xf = x.astype(jnp.float32)
    mu = jnp.mean(xf, axis=-1, keepdims=True)
    var = jnp.mean(jnp.square(xf - mu), axis=-1, keepdims=True)
    y = (xf - mu) * lax.rsqrt(var + EPS) * g.astype(jnp.float32) + b.astype(jnp.float32)
    return y.astype(x.dtype)


def _axial_angles(seq):
    rows = seq // GRID_W
    row = jnp.repeat(jnp.arange(rows, dtype=jnp.float32), GRID_W)
    col = jnp.tile(jnp.arange(GRID_W, dtype=jnp.float32), rows)
    half = ROPE_DIM // 2
    inv_freq = ROPE_THETA ** (-jnp.arange(0, half, 2, dtype=jnp.float32) / half)
    return row[:, None] * inv_freq[None, :], col[:, None] * inv_freq[None, :]


def _rope_half(x, ang):
    n = x.shape[-1] // 2
    x1, x2 = x[..., :n], x[..., n:]
    cos = jnp.cos(ang)[:, None, :]
    sin = jnp.sin(ang)[:, None, :]
    return jnp.concatenate([x1 * cos - x2 * sin, x2 * cos + x1 * sin], axis=-1)


def _rope2d(x, ang_row, ang_col):
    xf = x.astype(jnp.float32)
    h = x.shape[-1] // 2
    y = jnp.concatenate([_rope_half(xf[..., :h], ang_row), _rope_half(xf[..., h:], ang_col)], axis=-1)
    return y.astype(x.dtype)


def _dwconv(x, w):
    k, ch = w.shape
    return lax.conv_general_dilated(
        x, w[:, None, :].astype(x.dtype), window_strides=(1,), padding=[(k // 2, k // 2)],
        dimension_numbers=('NWC', 'WIO', 'NWC'), feature_group_count=ch)


def _pool_mixer(u, w_pool, scale):
    b, l, ch = u.shape
    uf = u.astype(jnp.float32)
    cs = jnp.concatenate([jnp.zeros_like(uf[:, :1]), jnp.cumsum(uf, axis=1)], axis=1)
    t = jnp.arange(l)
    parts = []
    for gi, w in enumerate(POOL_WINDOWS):
        sl = slice(gi * POOL_GC, (gi + 1) * POOL_GC)
        lo = jnp.clip(t - w // 2, 0, l)
        hi = jnp.clip(t - w // 2 + w, 0, l)
        csg = cs[:, :, sl]
        mean = (csg[:, hi] - csg[:, lo]) / (hi - lo).astype(jnp.float32)[:, None]
        parts.append(mean - uf[:, :, sl])
    p = jnp.stack(parts, axis=2).astype(u.dtype)
    y = jnp.einsum('blgc,gcd->blgd', p, w_pool).reshape(b, l, ch)
    return y * scale


def _mla_q(zq, q_norm, w_uq, angs):
    b, l, _ = zq.shape
    q = (_rmsnorm(zq, q_norm) @ w_uq).reshape(b, l, MLA_HEADS, MLA_QK)
    q_nope, q_rope = q[..., :MLA_NOPE], q[..., MLA_NOPE:]
    if angs is not None:
        q_rope = _rope2d(q_rope, *angs)
    return jnp.concatenate([q_nope, q_rope], axis=-1)


def _mixer_kv(z_kv, angs, kv_norm, w_ukv):
    b, l, _ = z_kv.shape
    ckv, k_rope, gk, gv = jnp.split(z_kv, KV_SPLITS, axis=-1)
    kv = (_rmsnorm(ckv, kv_norm) @ w_ukv).reshape(b, l, MLA_HEADS, MLA_NOPE + MLA_V)
    k_nope, mv = kv[..., :MLA_NOPE], kv[..., MLA_NOPE:]
    k_rope = k_rope[:, :, None, :]
    gk = gk.reshape(b, l, GQA_KV_HEADS, GQA_HD)
    gv = gv.reshape(b, l, GQA_KV_HEADS, GQA_HD)
    if angs is not None:
        k_rope = _rope2d(k_rope, *angs)
        gk = _rope2d(gk, *angs)
    mk = jnp.concatenate([k_nope, jnp.broadcast_to(k_rope, (b, l, MLA_HEADS, MLA_ROPE))], axis=-1)
    return mk, mv, gk, gv


def _mla_ctx_attn(q, k, v):
    b, l = q.shape[:2]
    s = jnp.einsum('blhd,bjhd->bhlj', q, k, preferred_element_type=jnp.float32) * MLA_QK ** -0.5
    p = jax.nn.softmax(s, axis=-1).astype(v.dtype)
    return jnp.einsum('bhlj,bjhd->blhd', p, v).reshape(b, l, -1)


def _mla_latent_attn(q, k, v):
    b, s, h, dq = q.shape
    nb = s // Q_BLOCK
    qb = q.reshape(b, nb, Q_BLOCK, h, dq).transpose(1, 0, 2, 3, 4)

    def one_block(qblk):
        sc = jnp.einsum('bqhd,bkhd->bhqk', qblk, k, preferred_element_type=jnp.float32) * MLA_QK ** -0.5
        p = jax.nn.softmax(sc, axis=-1).astype(v.dtype)
        return jnp.einsum('bhqk,bkhd->bqhd', p, v)

    o = lax.map(one_block, qb)
    return o.transpose(1, 0, 2, 3, 4).reshape(b, s, -1)


def _gqa_window_attn(q, k, v, kc, vc, sink):
    b, s = q.shape[:2]
    nb = s // BLOCK
    pad = ((0, 0), (BLOCK, BLOCK), (0, 0), (0, 0))
    kp = jnp.pad(k, pad).reshape(b, nb + 2, BLOCK, GQA_KV_HEADS, GQA_HD)
    vp = jnp.pad(v, pad).reshape(b, nb + 2, BLOCK, GQA_KV_HEADS, GQA_HD)
    kb = jnp.concatenate([kp[:, :-2], kp[:, 1:-1], kp[:, 2:]], axis=2)
    vb = jnp.concatenate([vp[:, :-2], vp[:, 1:-1], vp[:, 2:]], axis=2)
    qb = q.reshape(b, nb, BLOCK, GQA_KV_HEADS, GQA_GROUP, GQA_HD)
    scale = GQA_HD ** -0.5
    s_win = jnp.einsum('bnqkgd,bnjkd->bnkgqj', qb, kb, preferred_element_type=jnp.float32) * scale
    s_ctx = jnp.einsum('bnqkgd,bjkd->bnkgqj', qb, kc, preferred_element_type=jnp.float32) * scale
    blk = jnp.arange(nb)[:, None, None] * BLOCK
    qpos = blk + jnp.arange(BLOCK)[None, :, None]
    kpos = blk - BLOCK + jnp.arange(3 * BLOCK)[None, None, :]
    valid = (kpos >= 0) & (kpos < s) & (jnp.abs(qpos - kpos) <= WINDOW)
    s_win = jnp.where(valid[None, :, None, None], s_win, -1e30)
    s_sink = jnp.broadcast_to(sink.astype(jnp.float32)[None, None, :, :, None, None], s_win.shape[:-1] + (1,))
    p = jax.nn.softmax(jnp.concatenate([s_sink, s_ctx, s_win], axis=-1), axis=-1)
    lc = kc.shape[1]
    p_ctx = p[..., 1:1 + lc].astype(v.dtype)
    p_win = p[..., 1 + lc:].astype(v.dtype)
    o = (jnp.einsum('bnkgqj,bjkd->bnqkgd', p_ctx, vc)
         + jnp.einsum('bnkgqj,bnjkd->bnqkgd', p_win, vb))
    return o.reshape(b, s, -1)


def _gqa_ctx_attn(q, k, v, sink):
    b, l = q.shape[:2]
    sc = jnp.einsum('blkgd,bjkd->bkglj', q, k, preferred_element_type=jnp.float32) * GQA_HD ** -0.5
    s_sink = jnp.broadcast_to(sink.astype(jnp.float32)[None, :, :, None, None], sc.shape[:-1] + (1,))
    p = jax.nn.softmax(jnp.concatenate([s_sink, sc], axis=-1), axis=-1)[..., 1:].astype(v.dtype)
    return jnp.einsum('bkglj,bjkd->blkgd', p, v).reshape(b, l, -1)


def _mix_stream(z, own_kv, ctx_kv, angs, mla_q_norm, mla_w_uq, gqa_sink, conv_w, conv_b,
                conv_ln_g, conv_ln_b, pool_w, pool_scale, w_branch, w_out):
    b, l, _ = z.shape
    _, zq, zconv, zpool, zgate = jnp.split(z, IN_SPLITS, axis=-1)
    zq_mla, zq_gqa = jnp.split(zq, [MLA_Q_RANK], axis=-1)
    mk, mv, gk, gv = own_kv
    a, g = jnp.split(zconv, 2, axis=-1)
    u = _dwconv(a * jax.nn.sigmoid(g), conv_w) + conv_b
    o_conv = jax.nn.silu(_layernorm(u, conv_ln_g, conv_ln_b))
    qm = _mla_q(zq_mla, mla_q_norm, mla_w_uq, angs)
    qg = zq_gqa.reshape(b, l, GQA_HEADS, GQA_HD)
    if angs is not None:
        qg = _rope2d(qg, *angs)
    qg = qg.reshape(b, l, GQA_KV_HEADS, GQA_GROUP, GQA_HD)
    sink = gqa_sink.reshape(GQA_KV_HEADS, GQA_GROUP)
    if ctx_kv is None:
        o_mla = _mla_ctx_attn(qm, mk, mv)
        o_gqa = _gqa_ctx_attn(qg, gk, gv, sink)
    else:
        cmk, cmv, cgk, cgv = ctx_kv
        o_mla = _mla_latent_attn(qm, jnp.concatenate([cmk, mk], axis=1), jnp.concatenate([cmv, mv], axis=1))
        o_gqa = _gqa_window_attn(qg, gk, gv, cgk, cgv, sink)
    o_pool = _pool_mixer(zpool, pool_w, pool_scale)
    gates = jax.nn.sigmoid(zgate.reshape(b, l, N_BRANCH, D_MODEL))
    branches = (o_conv, o_mla, o_gqa, o_pool)
    y = gates[:, :, 0] * (branches[0] @ w_branch[0])
    for n in range(1, N_BRANCH):
        y = y + gates[:, :, n] * (branches[n] @ w_branch[n])
    return y @ w_out


def _conv_ffn(h, w_up, conv_w, w_down):
    u = _dwconv(h @ w_up, conv_w)
    a, g = jnp.split(u, 2, axis=-1)
    return (jax.nn.silu(g) * a) @ w_down


def _layer(x, ctx, c, c_ctx, angs, norm1_g, norm2_g, w_ada, b_ada, w_in, mla_q_norm, mla_w_uq,
           mla_kv_norm, mla_w_ukv, gqa_sink, conv_w, conv_b, conv_ln_g, conv_ln_b, pool_w, pool_scale,
           w_branch, w_out, ffn_w_up, ffn_conv_w, ffn_w_down, last):
    b = x.shape[0]
    mod_x = (jax.nn.silu(c) @ w_ada + b_ada).reshape(b, N_MOD, D_MODEL)[:, :, None, :]
    mod_c = (jax.nn.silu(c_ctx) @ w_ada + b_ada).reshape(N_MOD, D_MODEL)
    sh1, sc1, gt1, sh2, sc2, gt2 = (mod_x[:, k] for k in range(N_MOD))
    csh1, csc1, cgt1, csh2, csc2, cgt2 = (mod_c[k] for k in range(N_MOD))

    hx = _rmsnorm(x, norm1_g) * (1 + sc1) + sh1
    hc = _rmsnorm(ctx, norm1_g) * (1 + csc1) + csh1
    zx = hx @ w_in
    zc = hc @ (w_in[:, :KV_COLS] if last else w_in)
    kv_c = _mixer_kv(zc[..., :KV_COLS], None, mla_kv_norm, mla_w_ukv)
    kv_x = _mixer_kv(zx[..., :KV_COLS], angs, mla_kv_norm, mla_w_ukv)
    mix_args = (mla_q_norm, mla_w_uq, gqa_sink, conv_w, conv_b, conv_ln_g, conv_ln_b,
                pool_w, pool_scale, w_branch, w_out)

    x = x + gt1 * _mix_stream(zx, kv_x, kv_c, angs, *mix_args)
    x = x + gt2 * _conv_ffn(_rmsnorm(x, norm2_g) * (1 + sc2) + sh2, ffn_w_up, ffn_conv_w, ffn_w_down)
    if not last:
        ctx = ctx + cgt1 * _mix_stream(zc, kv_c, None, None, *mix_args)
        ctx = ctx + cgt2 * _conv_ffn(_rmsnorm(ctx, norm2_g) * (1 + csc2) + csh2, ffn_w_up, ffn_conv_w, ffn_w_down)
    return x, ctx


def setup_inputs(seed: int = 0) -> dict:
    key = jax.random.key(seed)
    ks = iter(jax.random.split(key, 32))

    def nrm(shape, scale):
        return jax.random.normal(next(ks), shape, jnp.float32) * scale

    def gain(shape):
        return 1.0 + nrm(shape, 0.05)

    L = DEPTH
    return {
        'x': nrm((BATCH, SEQ, D_MODEL), 1.0),
        'c': nrm((BATCH, D_MODEL), 1.0),
        'ctx': nrm((BATCH, CTX_LEN, D_MODEL), 1.0),
        'c_ctx': nrm((D_MODEL,), 1.0),
        'norm1_g': gain((L, D_MODEL)),
        'norm2_g': gain((L, D_MODEL)),
        'w_ada': nrm((L, D_MODEL, N_MOD * D_MODEL), 0.5 * D_MODEL ** -0.5),
        'b_ada': nrm((L, N_MOD * D_MODEL), 0.02),
        'w_in': nrm((L, D_MODEL, IN_COLS), D_MODEL ** -0.5),
        'mla_q_norm': gain((L, MLA_Q_RANK)),
        'mla_w_uq': nrm((L, MLA_Q_RANK, MLA_HEADS * MLA_QK), MLA_Q_RANK ** -0.5),
        'mla_kv_norm': gain((L, MLA_KV_RANK)),
        'mla_w_ukv': nrm((L, MLA_KV_RANK, MLA_HEADS * (MLA_NOPE + MLA_V)), MLA_KV_RANK ** -0.5),
        'gqa_sink': nrm((L, GQA_HEADS), 0.5),
        'conv_w': nrm((L, CONV_K, CONV_CH), CONV_K ** -0.5),
        'conv_b': nrm((L, CONV_CH), 0.02),
        'conv_ln_g': gain((L, CONV_CH)),
        'conv_ln_b': nrm((L, CONV_CH), 0.02),
        'pool_w': nrm((L, POOL_GROUPS, POOL_GC, POOL_GC), POOL_GC ** -0.5),
        'pool_scale': gain((L, POOL_CH)),
        'w_branch': nrm((L, N_BRANCH, BRANCH_W, D_MODEL), BRANCH_W ** -0.5),
        'w_out': nrm((L, D_MODEL, D_MODEL), D_MODEL ** -0.5),
        'ffn_w_up': nrm((L, D_MODEL, 2 * D_FF), D_MODEL ** -0.5),
        'ffn_conv_w': nrm((L, FFN_CONV_K, 2 * D_FF), FFN_CONV_K ** -0.5),
        'ffn_w_down': nrm((L, D_FF, D_MODEL), D_FF ** -0.5),
        'final_norm_g': gain((D_MODEL,)),
    }


def reference(x, c, ctx, c_ctx, norm1_g, norm2_g, w_ada, b_ada, w_in, mla_q_norm, mla_w_uq,
              mla_kv_norm, mla_w_ukv, gqa_sink, conv_w, conv_b, conv_ln_g, conv_ln_b, pool_w,
              pool_scale, w_branch, w_out, ffn_w_up, ffn_conv_w, ffn_w_down, final_norm_g):
    angs = _axial_angles(x.shape[1])
    for i in range(DEPTH):
        x, ctx = _layer(
            x, ctx, c, c_ctx, angs, norm1_g[i], norm2_g[i], w_ada[i], b_ada[i], w_in[i],
            mla_q_norm[i], mla_w_uq[i], mla_kv_norm[i], mla_w_ukv[i], gqa_sink[i],
            conv_w[i], conv_b[i], conv_ln_g[i], conv_ln_b[i], pool_w[i], pool_scale[i],
            w_branch[i], w_out[i], ffn_w_up[i], ffn_conv_w[i], ffn_w_down[i], i == DEPTH - 1)
    return _rmsnorm(x, final_norm_g)
```

```python
import functools

import jax
import jax.numpy as jnp
from jax import lax
from jax.experimental import pallas as pl
from jax.experimental.pallas import tpu as pltpu

F32 = jnp.float32
BF16 = jnp.bfloat16

D_MODEL = 2048
DEPTH = 4
GRID_W = 64
EPS = 1e-6
ROPE_THETA = 10000.0
ROPE_DIM = 64
MLA_HEADS = 4
MLA_NOPE = 128
MLA_V = 128
MLA_RANK = 512
MLA_QK = MLA_NOPE + ROPE_DIM
GQA_HEADS = 8
GQA_KV_HEADS = 2
GQA_GROUP = GQA_HEADS // GQA_KV_HEADS
GQA_HD = 64
WINDOW = 128
CONV_CH = 512
CONV_K = 31
POOL_WINDOWS = (2, 4, 8, 16)
POOL_GC = 128
POOL_CH = 512
N_BRANCH = 4
BRANCH_W = 512
D_FF = 5504
N_MOD = 6

LANES = 128
VMEM_LIMIT_BYTES = 56 * 1024 * 1024

ZS_CONV = 0
ZS_POOL = 1024
ZS_CKV = 1536
ZS_CQ = 2048
ZS_GQ = 2560
ZS_GK = 3072
ZS_GV = 3200
ZS_KR = 3328
ZS_COLS = 3456
MLA_SLOT = 256
FF_PAD = 5632
HALO = 16
CONV_PAD = 16
POOL_REACH = 8
MIX_ROWS = 32

_SH1, _SC1, _GT1, _SH2, _SC2, _GT2 = range(N_MOD)


def _cparams(*sem):
    return pltpu.CompilerParams(dimension_semantics=sem, vmem_limit_bytes=VMEM_LIMIT_BYTES)


def _dot(a, b):
    return jnp.dot(a, b, preferred_element_type=F32)


def _dot_nt(a, b):
    return lax.dot_general(a, b, (((1,), (1,)), ((), ())), preferred_element_type=F32)


def _sigmoid(v):
    return 1.0 / (1.0 + jnp.exp(-v))


def _ada_kernel(c_ref, w_ref, b_ref, o_ref):
    c = c_ref[...]
    s = (c * _sigmoid(c)).astype(BF16)
    o_ref[0] = _dot(s, w_ref[0].astype(BF16)) + b_ref[0]


def _ada_mod(cs, w_ada, b_ada):
    depth, dm, n = w_ada.shape
    tn = 1024
    return pl.pallas_call(
        _ada_kernel,
        grid=(depth, n // tn),
        in_specs=[pl.BlockSpec((cs.shape[0], dm), lambda l, j: (0, 0)),
                  pl.BlockSpec((1, dm, tn), lambda l, j: (l, 0, j)),
                  pl.BlockSpec((1, 1, tn), lambda l, j: (l, 0, j))],
        out_specs=pl.BlockSpec((1, cs.shape[0], tn), lambda l, j: (l, 0, j)),
        out_shape=jax.ShapeDtypeStruct((depth, cs.shape[0], n), F32),
        compiler_params=_cparams("parallel", "parallel"),
        name="ada_mod",
    )(cs, w_ada, b_ada.reshape(depth, 1, n))


def _nm_matmul_kernel(x_ref, g_ref, mod_ref, w_ref, o_ref, h_ref, *, sh_row, sc_row, sigmoid):
    @pl.when(pl.program_id(1) == 0)
    def _():
        x = x_ref[...]
        y = x * lax.rsqrt(jnp.mean(x * x, axis=-1, keepdims=True) + EPS) * g_ref[...]
        h = y * (1.0 + mod_ref[0, sc_row:sc_row + 1, :]) + mod_ref[0, sh_row:sh_row + 1, :]
        h_ref[...] = h.astype(BF16)

    acc = _dot(h_ref[...], w_ref[...])
    if sigmoid:
        acc = _sigmoid(acc)
    o_ref[...] = acc.astype(o_ref.dtype)


def _nm_matmul(x, gain, mod, rows_per_mod, w, *, sh_row, sc_row, sigmoid=False, tm, tn, name):
    t, dm = x.shape
    n = w.shape[1]
    assert t % tm == 0 and n % tn == 0 and rows_per_mod % tm == 0
    per = rows_per_mod // tm
    return pl.pallas_call(
        functools.partial(_nm_matmul_kernel, sh_row=sh_row, sc_row=sc_row, sigmoid=sigmoid),
        grid=(t // tm, n // tn),
        in_specs=[pl.BlockSpec((tm, dm), lambda i, j: (i, 0)),
                  pl.BlockSpec((1, dm), lambda i, j: (0, 0)),
                  pl.BlockSpec((1, 8, dm), lambda i, j: (i // per, 0, 0)),
                  pl.BlockSpec((dm, tn), lambda i, j: (0, j))],
        out_specs=pl.BlockSpec((tm, tn), lambda i, j: (i, j)),
        out_shape=jax.ShapeDtypeStruct((t, n), BF16),
        scratch_shapes=[pltpu.VMEM((tm, dm), BF16)],
        compiler_params=_cparams("parallel", "arbitrary"),
        name=name,
    )(x, gain, mod, w)


def _rope(v, cos, sa, sb):
    return v * cos + pltpu.roll(v, LANES - 16, 1) * sa + pltpu.roll(v, 16, 1) * sb


def _rms(v, g):
    return v * lax.rsqrt(jnp.mean(v * v, axis=-1, keepdims=True) + EPS) * g


def _prep_kernel(ckv_ref, cq_ref, gq_ref, tail_ref, cos_ref, sa_ref, sb_ref, kvn_ref, qn_ref,
                 wk_ref, wv_ref, wq_ref, e_ref, q_out, k_out, v_out, gq_out, gk_out):
    cos, sa, sb = cos_ref[...], sa_ref[...], sb_ref[...]
    nkv = _rms(ckv_ref[...].astype(F32), kvn_ref[...]).astype(BF16)
    kr = _rope(tail_ref[:, 2 * LANES:3 * LANES].astype(F32), cos, sa, sb).astype(BF16)
    k_out[...] = (_dot(nkv, wk_ref[...]) + _dot(kr, e_ref[...])).astype(BF16)
    v_out[...] = _dot(nkv, wv_ref[...]).astype(BF16)

    nq = _rms(cq_ref[...].astype(F32), qn_ref[...]).astype(BF16)
    q = _dot(nq, wq_ref[...])
    scale = MLA_QK ** -0.5
    for h in range(MLA_HEADS):
        lo = h * MLA_SLOT
        q_out[:, lo:lo + LANES] = (q[:, lo:lo + LANES] * scale).astype(BF16)
        q_out[:, lo + LANES:lo + 2 * LANES] = (_rope(q[:, lo + LANES:lo + 2 * LANES], cos, sa, sb) * scale).astype(BF16)

    for c in range(GQA_HEADS * GQA_HD // LANES):
        sl = slice(c * LANES, (c + 1) * LANES)
        gq_out[:, sl] = (_rope(gq_ref[:, sl].astype(F32), cos, sa, sb) * GQA_HD ** -0.5).astype(BF16)
    gk_out[...] = _rope(tail_ref[:, 0:LANES].astype(F32), cos, sa, sb).astype(BF16)


def _prep(zs, tabs, tab_period, kvn, qn, wk, wv, wq, e, *, tm):
    t = zs.shape[0]
    assert t % tm == 0
    cb = lambda off, w: pl.BlockSpec((tm, w), lambda i: (i, off // w))
    tab = pl.BlockSpec((tm, LANES), lambda i: (i % tab_period, 0))
    full = lambda a: pl.BlockSpec(a.shape, lambda i: (0,) * a.ndim)
    row = lambda w: pl.BlockSpec((tm, w), lambda i: (i, 0))
    widths = (MLA_HEADS * MLA_SLOT, MLA_HEADS * MLA_SLOT, MLA_HEADS * MLA_V, GQA_HEADS * GQA_HD, GQA_KV_HEADS * GQA_HD)
    return pl.pallas_call(
        _prep_kernel,
        grid=(t // tm,),
        in_specs=[cb(ZS_CKV, MLA_RANK), cb(ZS_CQ, MLA_RANK), cb(ZS_GQ, GQA_HEADS * GQA_HD), cb(ZS_GK, 3 * LANES),
                  tab, tab, tab, full(kvn), full(qn), full(wk), full(wv), full(wq), full(e)],
        out_specs=[row(w) for w in widths],
        out_shape=[jax.ShapeDtypeStruct((t, w), BF16) for w in widths],
        compiler_params=_cparams("parallel"),
        name="qkv_prep",
    )(zs, zs, zs, zs, *tabs, kvn, qn, wk, wv, wq, e)


def _mla_kernel(*refs, has_x):
    if has_x:
        q_ref, kc_ref, vc_ref, kx_ref, vx_ref, o_ref = refs
    else:
        q_ref, kc_ref, vc_ref, o_ref = refs
    for h in range(MLA_HEADS):
        ks = slice(h * MLA_SLOT, (h + 1) * MLA_SLOT)
        vs = slice(h * MLA_V, (h + 1) * MLA_V)
        qh = q_ref[:, ks]
        sc = _dot_nt(qh, kc_ref[:, ks])
        m = jnp.max(sc, axis=-1, keepdims=True)
        if has_x:
            sx = _dot_nt(qh, kx_ref[:, ks])
            m = jnp.maximum(m, jnp.max(sx, axis=-1, keepdims=True))
        pc = jnp.exp(sc - m)
        den = jnp.sum(pc, axis=-1, keepdims=True)
        o = _dot(pc.astype(BF16), vc_ref[:, vs])
        if has_x:
            px = jnp.exp(sx - m)
            den = den + jnp.sum(px, axis=-1, keepdims=True)
            o = o + _dot(px.astype(BF16), vx_ref[:, vs])
        o_ref[:, vs] = (o / den).astype(o_ref.dtype)


def _mla_attn(q, kc, vc, kx, vx, *, nb, tq):
    has_x = kx is not None
    lq = q.shape[0] // nb
    lc = kc.shape[0] // nb
    nq = lq // tq
    kw, vw = MLA_HEADS * MLA_SLOT, MLA_HEADS * MLA_V
    in_specs = [pl.BlockSpec((tq, kw), lambda b, i: (b * nq + i, 0)),
                pl.BlockSpec((lc, kw), lambda b, i: (b, 0)),
                pl.BlockSpec((lc, vw), lambda b, i: (b, 0))]
    args = [q, kc, vc]
    if has_x:
        in_specs += [pl.BlockSpec((lq, kw), lambda b, i: (b, 0)), pl.BlockSpec((lq, vw), lambda b, i: (b, 0))]
        args += [kx, vx]
    return pl.pallas_call(
        functools.partial(_mla_kernel, has_x=has_x),
        grid=(nb, nq),
        in_specs=in_specs,
        out_specs=pl.BlockSpec((tq, vw), lambda b, i: (b * nq + i, 0)),
        out_shape=jax.ShapeDtypeStruct((q.shape[0], vw), BF16),
        compiler_params=_cparams("parallel", "parallel"),
        name="mla_attn_x" if has_x else "mla_attn_c",
    )(*args)


def _gqa_kernel(*refs, has_x, lx):
    if has_x:
        q_ref, kc_ref, vc_ref, kx_ref, vx_ref, sink_ref, o_ref = refs
        tq = q_ref.shape[0]
        slab = tq + 2 * WINDOW
        q0 = pl.program_id(1) * tq
        start = pl.multiple_of(jnp.clip(q0 - WINDOW, 0, lx - slab), LANES)
        kslab = kx_ref[pl.ds(start, slab), :]
        vslab = vx_ref[pl.ds(start, slab), :]
        qpos = q0 + lax.broadcasted_iota(jnp.int32, (tq, slab), 0)
        kpos = start + lax.broadcasted_iota(jnp.int32, (tq, slab), 1)
        valid = jnp.abs(qpos - kpos) <= WINDOW
    else:
        q_ref, kc_ref, vc_ref, sink_ref, o_ref = refs
    for h in range(GQA_HEADS):
        kv = slice((h // GQA_GROUP) * GQA_HD, (h // GQA_GROUP + 1) * GQA_HD)
        hs = slice(h * GQA_HD, (h + 1) * GQA_HD)
        qh = q_ref[:, hs]
        sink = sink_ref[h:h + 1, 0:1]
        sc = _dot_nt(qh, kc_ref[:, kv])
        m = jnp.maximum(jnp.max(sc, axis=-1, keepdims=True), sink)
        if has_x:
            sw = jnp.where(valid, _dot_nt(qh, kslab[:, kv]), -1e30)
            m = jnp.maximum(m, jnp.max(sw, axis=-1, keepdims=True))
        pc = jnp.exp(sc - m)
        den = jnp.sum(pc, axis=-1, keepdims=True) + jnp.exp(sink - m)
        o = _dot(pc.astype(BF16), vc_ref[:, kv])
        if has_x:
            pw = jnp.exp(sw - m)
            den = den + jnp.sum(pw, axis=-1, keepdims=True)
            o = o + _dot(pw.astype(BF16), vslab[:, kv])
        o_ref[:, hs] = (o / den).astype(o_ref.dtype)


def _gqa_attn(gq, gkc, zsc, gkx, zsx, sink, *, nb, tq):
    has_x = gkx is not None
    lq = gq.shape[0] // nb
    lc = gkc.shape[0] // nb
    nq = lq // tq
    qw, kw = GQA_HEADS * GQA_HD, GQA_KV_HEADS * GQA_HD
    vcol = ZS_GV // kw
    in_specs = [pl.BlockSpec((tq, qw), lambda b, i: (b * nq + i, 0)),
                pl.BlockSpec((lc, kw), lambda b, i: (b, 0)),
                pl.BlockSpec((lc, kw), lambda b, i: (b, vcol))]
    args = [gq, gkc, zsc]
    if has_x:
        assert lq >= tq + 2 * WINDOW
        in_specs += [pl.BlockSpec((lq, kw), lambda b, i: (b, 0)), pl.BlockSpec((lq, kw), lambda b, i: (b, vcol))]
        args += [gkx, zsx]
    in_specs.append(pl.BlockSpec(sink.shape, lambda b, i: (0, 0)))
    args.append(sink)
    return pl.pallas_call(
        functools.partial(_gqa_kernel, has_x=has_x, lx=lq),
        grid=(nb, nq),
        in_specs=in_specs,
        out_specs=pl.BlockSpec((tq, qw), lambda b, i: (b * nq + i, 0)),
        out_shape=jax.ShapeDtypeStruct((gq.shape[0], qw), BF16),
        compiler_params=_cparams("parallel", "parallel"),
        name="gqa_attn_x" if has_x else "gqa_attn_c",
    )(*args)


def _mixers_kernel(z_ref, cw_ref, cb_ref, lg_ref, lb_ref, pw_ref, ps_ref, oc_ref, op_ref, pa_ref, pp_ref, *, seq):
    zpad = jnp.zeros((CONV_PAD, CONV_CH), F32)
    a = z_ref[:, 0:CONV_CH].astype(F32)
    g = z_ref[:, CONV_CH:2 * CONV_CH].astype(F32)
    pa_ref[0:CONV_PAD, :] = zpad
    pa_ref[CONV_PAD + seq:2 * CONV_PAD + seq, :] = zpad
    pa_ref[CONV_PAD:CONV_PAD + seq, :] = a * _sigmoid(g)
    pp_ref[0:CONV_PAD, :] = zpad
    pp_ref[CONV_PAD + seq:2 * CONV_PAD + seq, :] = zpad
    pp_ref[CONV_PAD:CONV_PAD + seq, :] = z_ref[:, 2 * CONV_CH:2 * CONV_CH + POOL_CH].astype(F32)

    def shifted(win, j):
        return win if j == 0 else pltpu.roll(win, win.shape[0] - j, 0)

    def step(c, carry):
        r0 = pl.multiple_of(c * MIX_ROWS, MIX_ROWS)
        win = pa_ref[pl.ds(r0, MIX_ROWS + 2 * CONV_PAD), :]
        acc = jnp.zeros((MIX_ROWS, CONV_CH), F32)
        for j in range(8):
            wj = shifted(win, j)
            for q in range(2 * CONV_PAD // 8):
                k = 8 * q + j - (CONV_PAD - CONV_K // 2)
                if 0 <= k < CONV_K:
                    acc = acc + cw_ref[k:k + 1, :] * wj[8 * q:8 * q + MIX_ROWS, :]
        u = acc + cb_ref[...]
        mu = jnp.mean(u, axis=-1, keepdims=True)
        d = u - mu
        y = d * lax.rsqrt(jnp.mean(d * d, axis=-1, keepdims=True) + EPS) * lg_ref[...] + lb_ref[...]
        oc_ref[pl.ds(r0, MIX_ROWS), :] = (y * _sigmoid(y)).astype(BF16)
        pwin = pp_ref[pl.ds(r0 + (CONV_PAD - POOL_REACH), MIX_ROWS + 2 * POOL_REACH), :]
        sums = [jnp.zeros((MIX_ROWS, POOL_GC), F32) for _ in POOL_WINDOWS]
        for j in range(8):
            wj = shifted(pwin, j)
            for q in range(2 * POOL_REACH // 8):
                dlt = 8 * q + j - POOL_REACH
                for gi, w in enumerate(POOL_WINDOWS):
                    if -(w // 2) <= dlt < w - w // 2:
                        sums[gi] = sums[gi] + wj[8 * q:8 * q + MIX_ROWS, gi * POOL_GC:(gi + 1) * POOL_GC]
        t = r0 + lax.broadcasted_iota(jnp.int32, (MIX_ROWS, 1), 0)
        for gi, w in enumerate(POOL_WINDOWS):
            ls = slice(gi * POOL_GC, (gi + 1) * POOL_GC)
            cnt = (jnp.minimum(t - w // 2 + w, seq) - jnp.maximum(t - w // 2, 0)).astype(F32)
            p = sums[gi] / cnt - pwin[POOL_REACH:POOL_REACH + MIX_ROWS, ls]
            op_ref[pl.ds(r0, MIX_ROWS), ls] = (_dot(p.astype(BF16), pw_ref[gi]) * ps_ref[:, ls]).astype(BF16)
        return carry

    lax.fori_loop(0, seq // MIX_ROWS, step, 0)


def _mixers(zs, cw, cb, lg, lb, pw, ps, *, seq):
    t = zs.shape[0]
    assert t % seq == 0 and seq % MIX_ROWS == 0
    full = lambda a: pl.BlockSpec(a.shape, lambda s: (0,) * a.ndim)
    zw = 2 * CONV_CH + POOL_CH
    out = pl.BlockSpec((seq, CONV_CH), lambda s: (s, 0))
    return pl.pallas_call(
        functools.partial(_mixers_kernel, seq=seq),
        grid=(t // seq,),
        in_specs=[pl.BlockSpec((seq, zw), lambda s: (s, 0)), full(cw), full(cb), full(lg), full(lb), full(pw), full(ps)],
        out_specs=[out, out],
        out_shape=[jax.ShapeDtypeStruct((t, CONV_CH), BF16), jax.ShapeDtypeStruct((t, POOL_CH), BF16)],
        scratch_shapes=[pltpu.VMEM((seq + 2 * CONV_PAD, CONV_CH), F32), pltpu.VMEM((seq + 2 * CONV_PAD, POOL_CH), F32)],
        compiler_params=_cparams("parallel"),
        name="local_mixers",
    )(zs, cw, cb, lg, lb, pw, ps)


def _merge_kernel(oc_ref, om_ref, og_ref, op_ref, sg_ref, wb_ref, y_ref):
    acc = None
    for n, br in enumerate((oc_ref, om_ref, og_ref, op_ref)):
        term = sg_ref[:, n * D_MODEL:(n + 1) * D_MODEL].astype(F32) * _dot(br[...], wb_ref[n])
        acc = term if acc is None else acc + term
    y_ref[...] = acc.astype(BF16)


def _merge(branches, sg, wb, *, tm):
    t = sg.shape[0]
    assert t % tm == 0
    br = pl.BlockSpec((tm, BRANCH_W), lambda i: (i, 0))
    return pl.pallas_call(
        _merge_kernel,
        grid=(t // tm,),
        in_specs=[br, br, br, br, pl.BlockSpec((tm, N_BRANCH * D_MODEL), lambda i: (i, 0)),
                  pl.BlockSpec(wb.shape, lambda i: (0, 0, 0))],
        out_specs=pl.BlockSpec((tm, D_MODEL), lambda i: (i, 0)),
        out_shape=jax.ShapeDtypeStruct((t, D_MODEL), BF16),
        compiler_params=_cparams("parallel"),
        name="gated_merge",
    )(*branches, sg, wb)


def _resid_matmul_kernel(y_ref, w_ref, x_ref, mod_ref, o_ref, *, gt_row):
    o_ref[...] = x_ref[...] + mod_ref[0, gt_row:gt_row + 1, :] * _dot(y_ref[...], w_ref[...])


def _resid_matmul(y, w, x, mod, rows_per_mod, *, gt_row, tm):
    t, dm = x.shape
    assert t % tm == 0 and rows_per_mod % tm == 0
    per = rows_per_mod // tm
    return pl.pallas_call(
        functools.partial(_resid_matmul_kernel, gt_row=gt_row),
        grid=(t // tm,),
        in_specs=[pl.BlockSpec((tm, y.shape[1]), lambda i: (i, 0)),
                  pl.BlockSpec(w.shape, lambda i: (0, 0)),
                  pl.BlockSpec((tm, dm), lambda i: (i, 0)),
                  pl.BlockSpec((1, 8, dm), lambda i: (i // per, 0, 0))],
        out_specs=pl.BlockSpec((tm, dm), lambda i: (i, 0)),
        out_shape=jax.ShapeDtypeStruct((t, dm), F32),
        compiler_params=_cparams("parallel"),
        name="out_proj",
    )(y, w, x, mod)


def _ffn_down_kernel(ua_ref, ug_ref, uat_ref, uab_ref, ugt_ref, ugb_ref, cwa_ref, cwg_ref, wd_ref, x_ref, mod_ref,
                     o_ref, *, seq, gt_row):
    tm = ua_ref.shape[0]
    rid = lax.broadcasted_iota(jnp.int32, (tm, 1), 0)
    pos = (pl.program_id(0) * tm + rid) % seq
    first, last = pos == 0, pos == seq - 1

    def conv(u_ref, top_ref, bot_ref, cw_ref):
        u = u_ref[...].astype(F32)
        up = jnp.where(rid == 0, top_ref[HALO - 1:HALO, :].astype(F32), pltpu.roll(u, 1, 0))
        dn = jnp.where(rid == tm - 1, bot_ref[0:1, :].astype(F32), pltpu.roll(u, tm - 1, 0))
        up = jnp.where(first, 0.0, up)
        dn = jnp.where(last, 0.0, dn)
        return cw_ref[0:1, :] * up + cw_ref[1:2, :] * u + cw_ref[2:3, :] * dn

    a = conv(ua_ref, uat_ref, uab_ref, cwa_ref)
    g = conv(ug_ref, ugt_ref, ugb_ref, cwg_ref)
    act = (g * _sigmoid(g) * a).astype(BF16)
    part = mod_ref[0, gt_row:gt_row + 1, :] * _dot(act, wd_ref[...])

    @pl.when(pl.program_id(1) == 0)
    def _():
        o_ref[...] = x_ref[...] + part

    @pl.when(pl.program_id(1) > 0)
    def _():
        o_ref[...] += part


def _ffn_down(u, cw, wd, x, mod, rows_per_mod, *, seq, gt_row, tm, tk):
    t, dm = x.shape
    fp = wd.shape[0]
    assert t % tm == 0 and fp % tk == 0 and rows_per_mod % tm == 0 and tm % HALO == 0 and t % seq == 0
    per = rows_per_mod // tm
    nk = fp // tk
    hb = tm // HALO
    last_hb = t // HALO - 1
    main = lambda off: pl.BlockSpec((tm, tk), lambda i, k: (i, off + k))
    top = lambda off: pl.BlockSpec((HALO, tk), lambda i, k: (jnp.maximum(i * hb - 1, 0), off + k))
    bot = lambda off: pl.BlockSpec((HALO, tk), lambda i, k: (jnp.minimum((i + 1) * hb, last_hb), off + k))
    cws = lambda off: pl.BlockSpec((8, tk), lambda i, k: (0, off + k))
    return pl.pallas_call(
        functools.partial(_ffn_down_kernel, seq=seq, gt_row=gt_row),
        grid=(t // tm, nk),
        in_specs=[main(0), main(nk), top(0), bot(0), top(nk), bot(nk), cws(0), cws(nk),
                  pl.BlockSpec((tk, dm), lambda i, k: (k, 0)),
                  pl.BlockSpec((tm, dm), lambda i, k: (i, 0)),
                  pl.BlockSpec((1, 8, dm), lambda i, k: (i // per, 0, 0))],
        out_specs=pl.BlockSpec((tm, dm), lambda i, k: (i, 0)),
        out_shape=jax.ShapeDtypeStruct((t, dm), F32),
        compiler_params=_cparams("parallel", "arbitrary"),
        name="ffn_down",
    )(u, u, u, u, u, u, cw, cw, wd, x, mod)


def _final_norm_kernel(x_ref, g_ref, o_ref):
    o_ref[...] = _rms(x_ref[...], g_ref[...])


def _final_norm(x, g, *, tm):
    t, dm = x.shape
    return pl.pallas_call(
        _final_norm_kernel,
        grid=(t // tm,),
        in_specs=[pl.BlockSpec((tm, dm), lambda i: (i, 0)), pl.BlockSpec((1, dm), lambda i: (0, 0))],
        out_specs=pl.BlockSpec((tm, dm), lambda i: (i, 0)),
        out_shape=jax.ShapeDtypeStruct((t, dm), F32),
        compiler_params=_cparams("parallel"),
        name="final_norm",
    )(x, g)


def _rope_tables(seq):
    t = jnp.arange(seq)
    row = (t // GRID_W).astype(F32)
    col = (t % GRID_W).astype(F32)
    half = ROPE_DIM // 2
    inv_freq = ROPE_THETA ** (-jnp.arange(0, half, 2, dtype=F32) / half)
    d = jnp.arange(ROPE_DIM)
    ang = jnp.where((d < half)[None, :], row[:, None], col[:, None]) * inv_freq[d % (half // 2)][None, :]
    upper = ((d % half) < half // 2)[None, :]
    cos = jnp.cos(ang)
    sin = jnp.sin(ang)
    sa = jnp.where(upper, -sin, 0.0)
    sb = jnp.where(upper, 0.0, sin)
    rep = lambda a: jnp.tile(a, (1, LANES // ROPE_DIM)).astype(F32)
    return rep(cos), rep(sa), rep(sb)


def _identity_tables(rows):
    return jnp.ones((rows, LANES), F32), jnp.zeros((rows, LANES), F32), jnp.zeros((rows, LANES), F32)


def _layer_weights(w_in, mla_w_uq, mla_w_ukv, w_branch, w_out, ffn_w_up, ffn_conv_w, ffn_w_down):
    c = lambda a, b: w_in[:, a:b]
    w_small = jnp.concatenate([c(1856, 2880), c(2880, 3392), c(0, 512), c(832, 1344), c(1344, 1856), c(576, 704),
                               c(704, 832), c(512, 576), jnp.zeros((D_MODEL, 64), w_in.dtype)], axis=1).astype(BF16)
    w_gate = c(3392, 3392 + N_BRANCH * D_MODEL).astype(BF16)
    uq = mla_w_uq.reshape(MLA_RANK, MLA_HEADS, MLA_QK)
    wq = jnp.pad(uq, ((0, 0), (0, 0), (0, MLA_SLOT - MLA_QK))).reshape(MLA_RANK, MLA_HEADS * MLA_SLOT).astype(BF16)
    ukv = mla_w_ukv.reshape(MLA_RANK, MLA_HEADS, MLA_NOPE + MLA_V)
    wk = jnp.pad(ukv[:, :, :MLA_NOPE], ((0, 0), (0, 0), (0, MLA_SLOT - MLA_NOPE)))
    wk = wk.reshape(MLA_RANK, MLA_HEADS * MLA_SLOT).astype(BF16)
    wv = ukv[:, :, MLA_NOPE:].reshape(MLA_RANK, MLA_HEADS * MLA_V).astype(BF16)
    padc = FF_PAD - D_FF
    up = ffn_w_up.reshape(D_MODEL, 2, D_FF)
    w_up = jnp.pad(up, ((0, 0), (0, 0), (0, padc))).reshape(D_MODEL, 2 * FF_PAD).astype(BF16)
    cw = jnp.pad(ffn_conv_w.reshape(3, 2, D_FF), ((0, 5), (0, 0), (0, padc))).reshape(8, 2 * FF_PAD)
    w_down = jnp.pad(ffn_w_down, ((0, padc), (0, 0))).astype(BF16)
    return dict(w_small=w_small, w_gate=w_gate, wq=wq, wk=wk, wv=wv, w_branch=w_branch.astype(BF16),
                w_out=w_out.astype(BF16), w_up=w_up, ffn_cw=cw, w_down=w_down)


def _rope_placement():
    d = jnp.arange(LANES)[:, None]
    col = jnp.arange(MLA_HEADS * MLA_SLOT)[None, :]
    return ((col % MLA_SLOT == MLA_NOPE + d) & (d < ROPE_DIM)).astype(BF16)


def _stream_front(x, mod, rows_per_mod, lw, p, tabs, tab_period, *, tm, seq):
    zs = _nm_matmul(x, p["norm1_g"], mod, rows_per_mod, lw["w_small"], sh_row=_SH1, sc_row=_SC1,
                    tm=tm, tn=ZS_COLS // 3, name="in_proj_small")
    q, k, v, gq, gk = _prep(zs, tabs, tab_period, p["mla_kv_norm"], p["mla_q_norm"], lw["wk"], lw["wv"], lw["wq"],
                            p["rope_place"], tm=min(tm, 512))
    return zs, q, k, v, gq, gk


def _stream_back(x, mod, rows_per_mod, lw, p, zs, o_mla, o_gqa, *, tm, seq):
    o_conv, o_pool = _mixers(zs, p["conv_w"], p["conv_b"], p["conv_ln_g"], p["conv_ln_b"], p["pool_w"],
                             p["pool_scale"], seq=seq)
    sg = _nm_matmul(x, p["norm1_g"], mod, rows_per_mod, lw["w_gate"], sh_row=_SH1, sc_row=_SC1, sigmoid=True,
                    tm=tm, tn=1024, name="in_proj_gates")
    y = _merge((o_conv, o_mla, o_gqa, o_pool), sg, lw["w_branch"], tm=min(tm, 256))
    x = _resid_matmul(y, lw["w_out"], x, mod, rows_per_mod, gt_row=_GT1, tm=min(tm, 512))
    u = _nm_matmul(x, p["norm2_g"], mod, rows_per_mod, lw["w_up"], sh_row=_SH2, sc_row=_SC2,
                   tm=tm, tn=1024, name="ffn_up")
    return _ffn_down(u, lw["ffn_cw"], lw["w_down"], x, mod, rows_per_mod, seq=seq, gt_row=_GT2, tm=tm, tk=512)


def kernel(x, c, ctx, c_ctx, norm1_g, norm2_g, w_ada, b_ada, w_in, mla_q_norm, mla_w_uq, mla_kv_norm, mla_w_ukv,
           gqa_sink, conv_w, conv_b, conv_ln_g, conv_ln_b, pool_w, pool_scale, w_branch, w_out, ffn_w_up,
           ffn_conv_w, ffn_w_down, final_norm_g):
    nb, seq, dm = x.shape
    lc = ctx.shape[1]
    tx, tc = nb * seq, nb * lc
    tm_x = min(1024, seq)
    tm_c = min(1024, tc)

    cs = jnp.zeros((16, dm), F32).at[:nb].set(c).at[nb].set(c_ctx)
    mod = _ada_mod(cs, w_ada, b_ada).reshape(DEPTH, 16, N_MOD, dm)
    mod = jnp.pad(mod, ((0, 0), (0, 0), (0, 8 - N_MOD), (0, 0)))

    tabs_x = _rope_tables(seq)
    tabs_c = _identity_tables(min(tm_c, 512))
    place = _rope_placement()

    xs = x.reshape(tx, dm)
    cx = ctx.reshape(tc, dm)
    for l in range(DEPTH):
        last = l == DEPTH - 1
        lw = _layer_weights(w_in[l], mla_w_uq[l], mla_w_ukv[l], w_branch[l], w_out[l], ffn_w_up[l], ffn_conv_w[l],
                            ffn_w_down[l])
        p = dict(norm1_g=norm1_g[l][None], norm2_g=norm2_g[l][None], mla_kv_norm=mla_kv_norm[l][None],
                 mla_q_norm=mla_q_norm[l][None], rope_place=place, conv_w=conv_w[l], conv_b=conv_b[l][None],
                 conv_ln_g=conv_ln_g[l][None], conv_ln_b=conv_ln_b[l][None], pool_w=pool_w[l].astype(BF16),
                 pool_scale=pool_scale[l][None])
        sink = jnp.broadcast_to(gqa_sink[l][:, None], (GQA_HEADS, LANES))
        mod_x, mod_c = mod[l, :nb], mod[l, nb:nb + 1]

        zs_c, q_c, k_c, v_c, gq_c, gk_c = _stream_front(cx, mod_c, tc, lw, p, tabs_c, 1, tm=tm_c, seq=lc)
        zs_x, q_x, k_x, v_x, gq_x, gk_x = _stream_front(xs, mod_x, seq, lw, p, tabs_x, seq // min(tm_x, 512),
                                                        tm=tm_x, seq=seq)
        o_mla = _mla_attn(q_x, k_c, v_c, k_x, v_x, nb=nb, tq=min(256, seq))
        o_gqa = _gqa_attn(gq_x, gk_c, zs_c, gk_x, zs_x, sink, nb=nb, tq=WINDOW)
        xs = _stream_back(xs, mod_x, seq, lw, p, zs_x, o_mla, o_gqa, tm=tm_x, seq=seq)
        if not last:
            o_mla_c = _mla_attn(q_c, k_c, v_c, None, None, nb=nb, tq=min(256, lc))
            o_gqa_c = _gqa_attn(gq_c, gk_c, zs_c, None, None, sink, nb=nb, tq=min(256, lc))
            cx = _stream_back(cx, mod_c, tc, lw, p, zs_c, o_mla_c, o_gqa_c, tm=tm_c, seq=lc)

    return _final_norm(xs, final_norm_g[None], tm=tm_x).reshape(nb, seq, dm)
```

```python
import functools

import jax
import jax.numpy as jnp
from jax import lax
from jax.experimental import pallas as pl
from jax.experimental.pallas import tpu as pltpu

F32 = jnp.float32
BF16 = jnp.bfloat16

D_MODEL = 2048
DEPTH = 4
GRID_W = 64
EPS = 1e-6
ROPE_THETA = 10000.0
ROPE_DIM = 64
MLA_HEADS = 4
MLA_NOPE = 128
MLA_V = 128
MLA_RANK = 512
MLA_QK = MLA_NOPE + ROPE_DIM
GQA_HEADS = 8
GQA_KV_HEADS = 2
GQA_GROUP = GQA_HEADS // GQA_KV_HEADS
GQA_HD = 64
WINDOW = 128
CONV_CH = 512
CONV_K = 31
POOL_WINDOWS = (2, 4, 8, 16)
POOL_GC = 128
POOL_CH = 512
N_BRANCH = 4
BRANCH_W = 512
D_FF = 5504
N_MOD = 6

LANES = 128
VMEM_LIMIT_BYTES = 56 * 1024 * 1024

ZS_CONV = 0
ZS_POOL = 1024
ZS_CKV = 1536
ZS_CQ = 2048
ZS_GQ = 2560
ZS_GK = 3072
ZS_GV = 3200
ZS_KR = 3328
ZS_COLS = 3456
MLA_SLOT = 256
FF_PAD = 5632
HALO = 16
CONV_PAD = 16
POOL_REACH = 8
MIX_ROWS = 32
GQA_ROWS = 256
GQA_UNROLL = 4

_SH1, _SC1, _GT1, _SH2, _SC2, _GT2 = range(N_MOD)


def _cparams(*sem):
    return pltpu.CompilerParams(dimension_semantics=sem, vmem_limit_bytes=VMEM_LIMIT_BYTES)


def _dot(a, b):
    return jnp.dot(a, b, preferred_element_type=F32)


def _dot_nt(a, b):
    return lax.dot_general(a, b, (((1,), (1,)), ((), ())), preferred_element_type=F32)


def _sigmoid(v):
    return 1.0 / (1.0 + jnp.exp(-v))


def _rms(v, g):
    return v * lax.rsqrt(jnp.mean(v * v, axis=-1, keepdims=True) + EPS) * g


def _layer_spec(arr, layer):
    zeros = (0,) * (arr.ndim - 1)
    return pl.BlockSpec((1,) + arr.shape[1:], lambda *_: (layer,) + zeros)


def _mod_spec(layer, cond, tm, dm):
    per, base = cond
    return pl.BlockSpec((1, 1, 8, dm), lambda i, *_: (layer, base + (i * tm) // per, 0, 0))


def _ada_kernel(c_ref, w_ref, b_ref, o_ref):
    c = c_ref[...]
    s = (c * _sigmoid(c)).astype(BF16)
    o_ref[0] = _dot(s, w_ref[0].astype(BF16)) + b_ref[0]


def _ada_mod(cs, w_ada, b_ada):
    depth, dm, n = w_ada.shape
    tn = 1024
    return pl.pallas_call(
        _ada_kernel,
        grid=(depth, n // tn),
        in_specs=[pl.BlockSpec((cs.shape[0], dm), lambda l, j: (0, 0)),
                  pl.BlockSpec((1, dm, tn), lambda l, j: (l, 0, j)),
                  pl.BlockSpec((1, 1, tn), lambda l, j: (l, 0, j))],
        out_specs=pl.BlockSpec((1, cs.shape[0], tn), lambda l, j: (l, 0, j)),
        out_shape=jax.ShapeDtypeStruct((depth, cs.shape[0], n), F32),
        compiler_params=_cparams("parallel", "parallel"),
        name="ada_mod",
    )(cs, w_ada, b_ada.reshape(depth, 1, n))


def _nm_matmul_kernel(x_ref, g_ref, mod_ref, w_ref, o_ref, h_ref, *, sh_row, sc_row, sigmoid):
    @pl.when(pl.program_id(1) == 0)
    def _():
        y = _rms(x_ref[...], g_ref[0])
        h = y * (1.0 + mod_ref[0, 0, sc_row:sc_row + 1, :]) + mod_ref[0, 0, sh_row:sh_row + 1, :]
        h_ref[...] = h.astype(BF16)

    acc = _dot(h_ref[...], w_ref[0])
    if sigmoid:
        acc = _sigmoid(acc)
    o_ref[...] = acc.astype(o_ref.dtype)


def _nm_matmul(x, gain, mod, cond, w, layer, *, sh_row, sc_row, sigmoid=False, tm, tn, name):
    t, dm = x.shape
    n = w.shape[2]
    assert t % tm == 0 and n % tn == 0
    return pl.pallas_call(
        functools.partial(_nm_matmul_kernel, sh_row=sh_row, sc_row=sc_row, sigmoid=sigmoid),
        grid=(t // tm, n // tn),
        in_specs=[pl.BlockSpec((tm, dm), lambda i, j: (i, 0)),
                  _layer_spec(gain, layer),
                  _mod_spec(layer, cond, tm, dm),
                  pl.BlockSpec((1, dm, tn), lambda i, j: (layer, 0, j))],
        out_specs=pl.BlockSpec((tm, tn), lambda i, j: (i, j)),
        out_shape=jax.ShapeDtypeStruct((t, n), BF16),
        scratch_shapes=[pltpu.VMEM((tm, dm), BF16)],
        compiler_params=_cparams("parallel", "arbitrary"),
        name=name,
    )(x, gain, mod, w)


def _rope(v, cos, sa, sb):
    return v * cos + pltpu.roll(v, LANES - 16, 1) * sa + pltpu.roll(v, 16, 1) * sb


def _prep_kernel(ckv_ref, cq_ref, gq_ref, tail_ref, cos_ref, sa_ref, sb_ref, kvn_ref, qn_ref,
                 wk_ref, wv_ref, wq_ref, e_ref, q_out, k_out, v_out, gq_out, gk_out):
    cos, sa, sb = cos_ref[...], sa_ref[...], sb_ref[...]
    nkv = _rms(ckv_ref[...].astype(F32), kvn_ref[0]).astype(BF16)
    kr = _rope(tail_ref[:, 2 * LANES:3 * LANES].astype(F32), cos, sa, sb).astype(BF16)
    k_out[...] = (_dot(nkv, wk_ref[0]) + _dot(kr, e_ref[...])).astype(BF16)
    v_out[...] = _dot(nkv, wv_ref[0]).astype(BF16)

    nq = _rms(cq_ref[...].astype(F32), qn_ref[0]).astype(BF16)
    q = _dot(nq, wq_ref[0])
    scale = MLA_QK ** -0.5
    for h in range(MLA_HEADS):
        lo = h * MLA_SLOT
        q_out[:, lo:lo + LANES] = (q[:, lo:lo + LANES] * scale).astype(BF16)
        q_out[:, lo + LANES:lo + 2 * LANES] = (_rope(q[:, lo + LANES:lo + 2 * LANES], cos, sa, sb) * scale).astype(BF16)

    for c in range(GQA_HEADS * GQA_HD // LANES):
        sl = slice(c * LANES, (c + 1) * LANES)
        gq_out[:, sl] = (_rope(gq_ref[:, sl].astype(F32), cos, sa, sb) * GQA_HD ** -0.5).astype(BF16)
    gk_out[...] = _rope(tail_ref[:, 0:LANES].astype(F32), cos, sa, sb).astype(BF16)


def _prep(zs, tabs, tab_period, kvn, qn, wk, wv, wq, e, layer, *, tm):
    t = zs.shape[0]
    assert t % tm == 0
    cb = lambda off, w: pl.BlockSpec((tm, w), lambda i: (i, off // w))
    tab = pl.BlockSpec((tm, LANES), lambda i: (i % tab_period, 0))
    lay = lambda a: _layer_spec(a, layer)
    row = lambda w: pl.BlockSpec((tm, w), lambda i: (i, 0))
    widths = (MLA_HEADS * MLA_SLOT, MLA_HEADS * MLA_SLOT, MLA_HEADS * MLA_V, GQA_HEADS * GQA_HD, GQA_KV_HEADS * GQA_HD)
    return pl.pallas_call(
        _prep_kernel,
        grid=(t // tm,),
        in_specs=[cb(ZS_CKV, MLA_RANK), cb(ZS_CQ, MLA_RANK), cb(ZS_GQ, GQA_HEADS * GQA_HD), cb(ZS_GK, 3 * LANES),
                  tab, tab, tab, lay(kvn), lay(qn), lay(wk), lay(wv), lay(wq),
                  pl.BlockSpec(e.shape, lambda i: (0, 0))],
        out_specs=[row(w) for w in widths],
        out_shape=[jax.ShapeDtypeStruct((t, w), BF16) for w in widths],
        compiler_params=_cparams("parallel"),
        name="qkv_prep",
    )(zs, zs, zs, zs, *tabs, kvn, qn, wk, wv, wq, e)


def _mla_kernel(*refs, has_x):
    if has_x:
        q_ref, kc_ref, vc_ref, kx_ref, vx_ref, o_ref = refs
    else:
        q_ref, kc_ref, vc_ref, o_ref = refs
    for h in range(MLA_HEADS):
        ks = slice(h * MLA_SLOT, (h + 1) * MLA_SLOT)
        vs = slice(h * MLA_V, (h + 1) * MLA_V)
        qh = q_ref[:, ks]
        sc = _dot_nt(qh, kc_ref[:, ks])
        m = jnp.max(sc, axis=-1, keepdims=True)
        if has_x:
            sx = _dot_nt(qh, kx_ref[:, ks])
            m = jnp.maximum(m, jnp.max(sx, axis=-1, keepdims=True))
        pc = jnp.exp(sc - m)
        den = jnp.sum(pc, axis=-1, keepdims=True)
        o = _dot(pc.astype(BF16), vc_ref[:, vs])
        if has_x:
            px = jnp.exp(sx - m)
            den = den + jnp.sum(px, axis=-1, keepdims=True)
            o = o + _dot(px.astype(BF16), vx_ref[:, vs])
        o_ref[:, vs] = (o / den).astype(o_ref.dtype)


def _mla_attn(q, kc, vc, kx, vx, *, nb, tq):
    has_x = kx is not None
    lq = q.shape[0] // nb
    lc = kc.shape[0] // nb
    nq = lq // tq
    kw, vw = MLA_HEADS * MLA_SLOT, MLA_HEADS * MLA_V
    in_specs = [pl.BlockSpec((tq, kw), lambda b, i: (b * nq + i, 0)),
                pl.BlockSpec((lc, kw), lambda b, i: (b, 0)),
                pl.BlockSpec((lc, vw), lambda b, i: (b, 0))]
    args = [q, kc, vc]
    if has_x:
        in_specs += [pl.BlockSpec((lq, kw), lambda b, i: (b, 0)), pl.BlockSpec((lq, vw), lambda b, i: (b, 0))]
        args += [kx, vx]
    return pl.pallas_call(
        functools.partial(_mla_kernel, has_x=has_x),
        grid=(nb, nq),
        in_specs=in_specs,
        out_specs=pl.BlockSpec((tq, vw), lambda b, i: (b * nq + i, 0)),
        out_shape=jax.ShapeDtypeStruct((q.shape[0], vw), BF16),
        compiler_params=_cparams("parallel", "parallel"),
        name="mla_attn_x" if has_x else "mla_attn_c",
    )(*args)


def _gqa_kernel(*refs, has_x, lx, tq):
    if has_x:
        q_ref, kc_ref, vc_ref, kx_ref, vx_ref, sink_ref, o_ref = refs
    else:
        q_ref, kc_ref, vc_ref, sink_ref, o_ref = refs
    rows = GQA_GROUP * tq
    lower = lax.broadcasted_iota(jnp.int32, (1, LANES), 1) < GQA_HD
    slab = 3 * WINDOW

    def chunk(c, carry):
        r0 = pl.multiple_of(c * GQA_ROWS, GQA_ROWS)
        q2 = q_ref[pl.ds(r0, GQA_ROWS), :]
        zero = jnp.zeros_like(q2)
        if has_x:
            t0 = pl.program_id(1) * tq + r0 // GQA_GROUP
            start = pl.multiple_of(jnp.clip((t0 - WINDOW) // WINDOW * WINDOW, 0, lx - slab), LANES)
            kslab = kx_ref[pl.ds(start, slab), :]
            vslab = vx_ref[pl.ds(start, slab), :]
            qpos = t0 + lax.broadcasted_iota(jnp.int32, (GQA_ROWS, slab), 0) // GQA_GROUP
            kpos = start + lax.broadcasted_iota(jnp.int32, (GQA_ROWS, slab), 1)
            valid = jnp.abs(qpos - kpos) <= WINDOW
        outs = []
        for half in range(GQA_KV_HEADS):
            qh = jnp.where(lower, q2, zero) if half == 0 else jnp.where(lower, zero, q2)
            sink = sink_ref[0, pl.ds(r0, GQA_ROWS), half:half + 1]
            sc = _dot_nt(qh, kc_ref[...])
            m = jnp.maximum(jnp.max(sc, axis=-1, keepdims=True), sink)
            if has_x:
                sw = jnp.where(valid, _dot_nt(qh, kslab), -1e30)
                m = jnp.maximum(m, jnp.max(sw, axis=-1, keepdims=True))
            pc = jnp.exp(sc - m)
            den = jnp.sum(pc, axis=-1, keepdims=True) + jnp.exp(sink - m)
            o = _dot(pc.astype(BF16), vc_ref[...])
            if has_x:
                pw = jnp.exp(sw - m)
                den = den + jnp.sum(pw, axis=-1, keepdims=True)
                o = o + _dot(pw.astype(BF16), vslab)
            outs.append(o / den)
        o_ref[pl.ds(r0, GQA_ROWS), :] = jnp.where(lower, outs[0], outs[1]).astype(o_ref.dtype)
        return carry

    lax.fori_loop(0, rows // GQA_ROWS, chunk, 0, unroll=GQA_UNROLL)


def _gqa_attn(gq, gkc, zsc, gkx, zsx, sink_rows, layer, *, nb, tq):
    has_x = gkx is not None
    lq = gq.shape[0] // nb
    lc = gkc.shape[0] // nb
    nq = lq // tq
    kw = GQA_KV_HEADS * GQA_HD
    vcol = ZS_GV // kw
    rows = GQA_GROUP * tq
    q2 = gq.reshape(gq.shape[0] * GQA_GROUP, kw)
    in_specs = [pl.BlockSpec((rows, kw), lambda b, i: (b * nq + i, 0)),
                pl.BlockSpec((lc, kw), lambda b, i: (b, 0)),
                pl.BlockSpec((lc, kw), lambda b, i: (b, vcol))]
    args = [q2, gkc, zsc]
    if has_x:
        assert lq >= 3 * WINDOW and tq % (GQA_ROWS // GQA_GROUP) == 0 and WINDOW % (GQA_ROWS // GQA_GROUP) == 0
        in_specs +=[pl.BlockSpec((lq, kw), lambda b, i: (b, 0)), pl.BlockSpec((lq, kw), lambda b, i: (b, vcol))]
        args += [gkx, zsx]
    assert sink_rows.shape[1] == rows
    in_specs.append(_layer_spec(sink_rows, layer))
    args.append(sink_rows)
    out = pl.pallas_call(
        functools.partial(_gqa_kernel, has_x=has_x, lx=lq, tq=tq),
        grid=(nb, nq),
        in_specs=in_specs,
        out_specs=pl.BlockSpec((rows, kw), lambda b, i: (b * nq + i, 0)),
        out_shape=jax.ShapeDtypeStruct(q2.shape, BF16),
        compiler_params=_cparams("parallel", "parallel"),
        name="gqa_attn_x" if has_x else "gqa_attn_c",
    )(*args)
    return out.reshape(gq.shape)


def _mixers_kernel(z_ref, cw_ref, cb_ref, lg_ref, lb_ref, pw_ref, ps_ref, oc_ref, op_ref, pa_ref, pp_ref, *, seq):
    zpad = jnp.zeros((CONV_PAD, CONV_CH), F32)
    a = z_ref[:, 0:CONV_CH].astype(F32)
    g = z_ref[:, CONV_CH:2 * CONV_CH].astype(F32)
    pa_ref[0:CONV_PAD, :] = zpad
    pa_ref[CONV_PAD + seq:2 * CONV_PAD + seq, :] = zpad
    pa_ref[CONV_PAD:CONV_PAD + seq, :] = a * _sigmoid(g)
    pp_ref[0:CONV_PAD, :] = zpad
    pp_ref[CONV_PAD + seq:2 * CONV_PAD + seq, :] = zpad
    pp_ref[CONV_PAD:CONV_PAD + seq, :] = z_ref[:, 2 * CONV_CH:2 * CONV_CH + POOL_CH].astype(F32)

    def shifted(win, j):
        return win if j == 0 else pltpu.roll(win, win.shape[0] - j, 0)

    def step(c, carry):
        r0 = pl.multiple_of(c * MIX_ROWS, MIX_ROWS)
        win = pa_ref[pl.ds(r0, MIX_ROWS + 2 * CONV_PAD), :]
        acc = jnp.zeros((MIX_ROWS, CONV_CH), F32)
        for j in range(8):
            wj = shifted(win, j)
            for q in range(2 * CONV_PAD // 8):
                k = 8 * q + j - (CONV_PAD - CONV_K // 2)
                if 0 <= k < CONV_K:
                    acc = acc + cw_ref[0, k:k + 1, :] * wj[8 * q:8 * q + MIX_ROWS, :]
        u = acc + cb_ref[0]
        mu = jnp.mean(u, axis=-1, keepdims=True)
        d = u - mu
        y = d * lax.rsqrt(jnp.mean(d * d, axis=-1, keepdims=True) + EPS) * lg_ref[0] + lb_ref[0]
        oc_ref[pl.ds(r0, MIX_ROWS), :] = (y * _sigmoid(y)).astype(BF16)
        pwin = pp_ref[pl.ds(r0 + (CONV_PAD - POOL_REACH), MIX_ROWS + 2 * POOL_REACH), :]
        sums = [jnp.zeros((MIX_ROWS, POOL_GC), F32) for _ in POOL_WINDOWS]
        for j in range(8):
            wj = shifted(pwin, j)
            for q in range(2 * POOL_REACH // 8):
                dlt = 8 * q + j - POOL_REACH
                for gi, w in enumerate(POOL_WINDOWS):
                    if -(w // 2) <= dlt < w - w // 2:
                        sums[gi] = sums[gi] + wj[8 * q:8 * q + MIX_ROWS, gi * POOL_GC:(gi + 1) * POOL_GC]
        t = r0 + lax.broadcasted_iota(jnp.int32, (MIX_ROWS, 1), 0)
        for gi, w in enumerate(POOL_WINDOWS):
            ls = slice(gi * POOL_GC, (gi + 1) * POOL_GC)
            cnt = (jnp.minimum(t - w // 2 + w, seq) - jnp.maximum(t - w // 2, 0)).astype(F32)
            p = sums[gi] / cnt - pwin[POOL_REACH:POOL_REACH + MIX_ROWS, ls]
            op_ref[pl.ds(r0, MIX_ROWS), ls] = (_dot(p.astype(BF16), pw_ref[0, gi]) * ps_ref[0][:, ls]).astype(BF16)
        return carry

    lax.fori_loop(0, seq // MIX_ROWS, step, 0)


def _mixers(zs, cw, cb, lg, lb, pw, ps, layer, *, seq):
    t = zs.shape[0]
    assert t % seq == 0 and seq % MIX_ROWS == 0
    lay = lambda a: _layer_spec(a, layer)
    zw = 2 * CONV_CH + POOL_CH
    out = pl.BlockSpec((seq, CONV_CH), lambda s: (s, 0))
    return pl.pallas_call(
        functools.partial(_mixers_kernel, seq=seq),
        grid=(t // seq,),
        in_specs=[pl.BlockSpec((seq, zw), lambda s: (s, 0)), lay(cw), lay(cb), lay(lg), lay(lb), lay(pw), lay(ps)],
        out_specs=[out, out],
        out_shape=[jax.ShapeDtypeStruct((t, CONV_CH), BF16), jax.ShapeDtypeStruct((t, POOL_CH), BF16)],
        scratch_shapes=[pltpu.VMEM((seq + 2 * CONV_PAD, CONV_CH), F32), pltpu.VMEM((seq + 2 * CONV_PAD, POOL_CH), F32)],
        compiler_params=_cparams("parallel"),
        name="local_mixers",
    )(zs, cw, cb, lg, lb, pw, ps)


def _merge_kernel(oc_ref, om_ref, og_ref, op_ref, sg_ref, wb_ref, y_ref):
    acc = None
    for n, br in enumerate((oc_ref, om_ref, og_ref, op_ref)):
        term = sg_ref[:, n * D_MODEL:(n + 1) * D_MODEL].astype(F32) * _dot(br[...], wb_ref[0, n])
        acc = term if acc is None else acc + term
    y_ref[...] = acc.astype(BF16)


def _merge(branches, sg, wb, layer, *, tm):
    t = sg.shape[0]
    assert t % tm == 0
    br = pl.BlockSpec((tm, BRANCH_W), lambda i: (i, 0))
    return pl.pallas_call(
        _merge_kernel,
        grid=(t // tm,),
        in_specs=[br, br, br, br, pl.BlockSpec((tm, N_BRANCH * D_MODEL), lambda i: (i, 0)), _layer_spec(wb, layer)],
        out_specs=pl.BlockSpec((tm, D_MODEL), lambda i: (i, 0)),
        out_shape=jax.ShapeDtypeStruct((t, D_MODEL), BF16),
        compiler_params=_cparams("parallel"),
        name="gated_merge",
    )(*branches, sg, wb)


def _resid_matmul_kernel(y_ref, w_ref, x_ref, mod_ref, o_ref, *, gt_row):
    o_ref[...] = x_ref[...] + mod_ref[0, 0, gt_row:gt_row + 1, :] * _dot(y_ref[...], w_ref[0])


def _resid_matmul(y, w, x, mod, cond, layer, *, gt_row, tm):
    t, dm = x.shape
    assert t % tm == 0
    return pl.pallas_call(
        functools.partial(_resid_matmul_kernel, gt_row=gt_row),
        grid=(t // tm,),
        in_specs=[pl.BlockSpec((tm, y.shape[1]), lambda i: (i, 0)),
                  _layer_spec(w, layer),
                  pl.BlockSpec((tm, dm), lambda i: (i, 0)),
                  _mod_spec(layer, cond, tm, dm)],
        out_specs=pl.BlockSpec((tm, dm), lambda i: (i, 0)),
        out_shape=jax.ShapeDtypeStruct((t, dm), F32),
        compiler_params=_cparams("parallel"),
        name="out_proj",
    )(y, w, x, mod)


def _ffn_kernel(x_ref, xt_ref, xb_ref, g_ref, mod_ref, wa_ref, wg_ref, cwa_ref, cwg_ref, wd_ref, *rest, seq, final):
    if final:
        fg_ref, o_ref, h_ref = rest
    else:
        o_ref, h_ref = rest
    i, k = pl.program_id(0), pl.program_id(1)
    tm = x_ref.shape[0]
    multi = tm > seq

    def normmod(x):
        y = _rms(x, g_ref[0])
        return (y * (1.0 + mod_ref[0, 0, _SC2:_SC2 + 1, :]) + mod_ref[0, 0, _SH2:_SH2 + 1, :]).astype(BF16)

    @pl.when(k == 0)
    def _():
        h_ref[HALO:HALO + tm, :] = normmod(x_ref[...])
        top, bot = normmod(xt_ref[...]), normmod(xb_ref[...])
        if not multi:
            top = jnp.where((i * tm) % seq == 0, jnp.zeros_like(top), top)
            bot = jnp.where(((i + 1) * tm) % seq == 0, jnp.zeros_like(bot), bot)
        h_ref[0:HALO, :] = top
        h_ref[HALO + tm:2 * HALO + tm, :] = bot
        o_ref[...] = x_ref[...]

    if multi:
        pos = lax.broadcasted_iota(jnp.int32, (tm, 1), 0) % seq
        first, last = pos == 0, pos == seq - 1

    def conv(u, cw_ref):
        prev, cur, nxt = u[HALO - 1:HALO - 1 + tm], u[HALO:HALO + tm], u[HALO + 1:HALO + 1 + tm]
        if multi:
            prev = jnp.where(first, 0.0, prev)
            nxt = jnp.where(last, 0.0, nxt)
        return cw_ref[0, 0:1, :] * prev + cw_ref[0, 1:2, :] * cur + cw_ref[0, 2:3, :] * nxt

    h = h_ref[...]
    a = conv(_dot(h, wa_ref[0]), cwa_ref)
    g = conv(_dot(h, wg_ref[0]), cwg_ref)
    act = (g * _sigmoid(g) * a).astype(BF16)
    o_ref[...] += mod_ref[0, 0, _GT2:_GT2 + 1, :] * _dot(act, wd_ref[0])

    if final:
        @pl.when(k == pl.num_programs(1) - 1)
        def _():
            o_ref[...] = _rms(o_ref[...], fg_ref[...])


def _ffn(x, gain, mod, cond, w_up, cw, w_down, layer, final_gain, *, seq, tm, tf):
    t, dm = x.shape
    fp = w_down.shape[1]
    assert t % tm == 0 and fp % tf == 0 and tm % HALO == 0 and t % seq == 0
    assert seq % tm == 0 or tm % seq == 0
    nk = fp // tf
    hb = tm // HALO
    last_hb = t // HALO - 1
    final = final_gain is not None
    in_specs = [pl.BlockSpec((tm, dm), lambda i, k: (i, 0)),
                pl.BlockSpec((HALO, dm), lambda i, k: (jnp.maximum(i * hb - 1, 0), 0)),
                pl.BlockSpec((HALO, dm), lambda i, k: (jnp.minimum((i + 1) * hb, last_hb), 0)),
                _layer_spec(gain, layer),
                _mod_spec(layer, cond, tm, dm),
                pl.BlockSpec((1, dm, tf), lambda i, k: (layer, 0, k)),
                pl.BlockSpec((1, dm, tf), lambda i, k: (layer, 0, nk + k)),
                pl.BlockSpec((1, 8, tf), lambda i, k: (layer, 0, k)),
                pl.BlockSpec((1, 8, tf), lambda i, k: (layer, 0, nk + k)),
                pl.BlockSpec((1, tf, dm), lambda i, k: (layer, k, 0))]
    args = [x, x, x, gain, mod, w_up, w_up, cw, cw, w_down]
    if final:
        in_specs.append(pl.BlockSpec((1, dm), lambda i, k: (0, 0)))
        args.append(final_gain)
    return pl.pallas_call(
        functools.partial(_ffn_kernel, seq=seq, final=final),
        grid=(t // tm, nk),
        in_specs=in_specs,
        out_specs=pl.BlockSpec((tm, dm), lambda i, k: (i, 0)),
        out_shape=jax.ShapeDtypeStruct((t, dm), F32),
        scratch_shapes=[pltpu.VMEM((tm + 2 * HALO, dm), BF16)],
        compiler_params=_cparams("parallel", "arbitrary"),
        name="conv_ffn",
    )(*args)


def _rope_tables(seq):
    t = jnp.arange(seq)
    row = (t // GRID_W).astype(F32)
    col = (t % GRID_W).astype(F32)
    half = ROPE_DIM // 2
    inv_freq = ROPE_THETA ** (-jnp.arange(0, half, 2, dtype=F32) / half)
    d = jnp.arange(ROPE_DIM)
    ang = jnp.where((d < half)[None, :], row[:, None], col[:, None]) * inv_freq[d % (half // 2)][None, :]
    upper = ((d % half) < half // 2)[None, :]
    cos = jnp.cos(ang)
    sin = jnp.sin(ang)
    sa = jnp.where(upper, -sin, 0.0)
    sb = jnp.where(upper, 0.0, sin)
    rep = lambda a: jnp.tile(a, (1, LANES // ROPE_DIM)).astype(F32)
    return rep(cos), rep(sa), rep(sb)


def _identity_tables(rows):
    return jnp.ones((rows, LANES), F32), jnp.zeros((rows, LANES), F32), jnp.zeros((rows, LANES), F32)


def _group_major(a, axis):
    shp = a.shape
    a = a.reshape(shp[:axis] + (GQA_KV_HEADS, GQA_GROUP, GQA_HD) + shp[axis + 1:])
    a = jnp.swapaxes(a, axis, axis + 1)
    return a.reshape(shp)


def _stacked_weights(w_in, mla_w_uq, mla_w_ukv, w_branch, w_out, ffn_w_up, ffn_conv_w, ffn_w_down, pool_w):
    depth = w_in.shape[0]
    c = lambda a, b: w_in[:, :, a:b]
    w_small = jnp.concatenate([c(1856, 2880), c(2880, 3392), c(0, 512), c(832, 1344), _group_major(c(1344, 1856), 2),
                               c(576, 704), c(704, 832), c(512, 576),
                               jnp.zeros((depth, D_MODEL, 64), w_in.dtype)], axis=2).astype(BF16)
    w_gate = c(3392, 3392 + N_BRANCH * D_MODEL).astype(BF16)
    uq = mla_w_uq.reshape(depth, MLA_RANK, MLA_HEADS, MLA_QK)
    wq = jnp.pad(uq, ((0, 0), (0, 0), (0, 0), (0, MLA_SLOT - MLA_QK)))
    wq = wq.reshape(depth, MLA_RANK, MLA_HEADS * MLA_SLOT).astype(BF16)
    ukv = mla_w_ukv.reshape(depth, MLA_RANK, MLA_HEADS, MLA_NOPE + MLA_V)
    wk = jnp.pad(ukv[..., :MLA_NOPE], ((0, 0), (0, 0), (0, 0), (0, MLA_SLOT - MLA_NOPE)))
    wk = wk.reshape(depth, MLA_RANK, MLA_HEADS * MLA_SLOT).astype(BF16)
    wv = ukv[..., MLA_NOPE:].reshape(depth, MLA_RANK, MLA_HEADS * MLA_V).astype(BF16)
    wb = jnp.concatenate([w_branch[:, :2], _group_major(w_branch[:, 2], 1)[:, None], w_branch[:, 3:]], axis=1)
    padc = FF_PAD - D_FF
    up = ffn_w_up.reshape(depth, D_MODEL, 2, D_FF)
    w_up = jnp.pad(up, ((0, 0), (0, 0), (0, 0), (0, padc))).reshape(depth, D_MODEL, 2 * FF_PAD).astype(BF16)
    cw = jnp.pad(ffn_conv_w.reshape(depth, 3, 2, D_FF), ((0, 0), (0, 5), (0, 0), (0, padc)))
    cw = cw.reshape(depth, 8, 2 * FF_PAD)
    w_down = jnp.pad(ffn_w_down, ((0, 0), (0, padc), (0, 0))).astype(BF16)
    return dict(w_small=w_small, w_gate=w_gate, wq=wq, wk=wk, wv=wv, w_branch=wb.astype(BF16),
                w_out=w_out.astype(BF16), w_up=w_up, ffn_cw=cw, w_down=w_down, pool_w=pool_w.astype(BF16))


def _rope_placement():
    d = jnp.arange(LANES)[:, None]
    col = jnp.arange(MLA_HEADS * MLA_SLOT)[None, :]
    return ((col % MLA_SLOT == MLA_NOPE + d) & (d < ROPE_DIM)).astype(BF16)


def _sink_rows(gqa_sink, tq):
    depth = gqa_sink.shape[0]
    s = gqa_sink.reshape(depth, GQA_KV_HEADS, GQA_GROUP).transpose(0, 2, 1)
    s = jnp.pad(s, ((0, 0), (0, 0), (0, LANES - GQA_KV_HEADS)))
    return jnp.tile(s, (1, tq, 1))


def _stream_front(x, mod, cond, sw, p, tabs, tab_period, layer, *, tm):
    zs = _nm_matmul(x, p["norm1_g"], mod, cond, sw["w_small"], layer, sh_row=_SH1, sc_row=_SC1,
                    tm=tm, tn=ZS_COLS // 3, name="in_proj_small")
    q, k, v, gq, gk = _prep(zs, tabs, tab_period, p["mla_kv_norm"], p["mla_q_norm"], sw["wk"], sw["wv"], sw["wq"],
                            p["rope_place"], layer, tm=min(tm, 512))
    return zs, q, k, v, gq, gk


def _stream_back(x, mod, cond, sw, p, zs, o_mla, o_gqa, layer, final_gain, *, tm, seq):
    o_conv, o_pool = _mixers(zs, p["conv_w"], p["conv_b"], p["conv_ln_g"], p["conv_ln_b"], sw["pool_w"],
                             p["pool_scale"], layer, seq=seq)
    sg = _nm_matmul(x, p["norm1_g"], mod, cond, sw["w_gate"], layer, sh_row=_SH1, sc_row=_SC1, sigmoid=True,
                    tm=tm, tn=1024, name="in_proj_gates")
    y = _merge((o_conv, o_mla, o_gqa, o_pool), sg, sw["w_branch"], layer, tm=min(tm, 256))
    x = _resid_matmul(y, sw["w_out"], x, mod, cond, layer, gt_row=_GT1, tm=min(tm, 512))
    return _ffn(x, p["norm2_g"], mod, cond, sw["w_up"], sw["ffn_cw"], sw["w_down"], layer, final_gain,
                seq=seq, tm=min(tm, 512), tf=512)


def kernel(x, c, ctx, c_ctx, norm1_g, norm2_g, w_ada, b_ada, w_in, mla_q_norm, mla_w_uq, mla_kv_norm, mla_w_ukv,
           gqa_sink, conv_w, conv_b, conv_ln_g, conv_ln_b, pool_w, pool_scale, w_branch, w_out, ffn_w_up,
           ffn_conv_w, ffn_w_down, final_norm_g):
    nb, seq, dm = x.shape
    lc = ctx.shape[1]
    tx, tc = nb * seq, nb * lc
    tm_x = min(1024, seq)
    tm_c = min(1024, tc)
    tq_x = min(256, seq - 2 * WINDOW)
    tq_c = min(256, lc)

    cs = jnp.zeros((16, dm), F32).at[:nb].set(c).at[nb].set(c_ctx)
    mod = _ada_mod(cs, w_ada, b_ada).reshape(DEPTH, 16, N_MOD, dm)
    mod = jnp.pad(mod, ((0, 0), (0, 0), (0, 8 - N_MOD), (0, 0)))

    sw = _stacked_weights(w_in, mla_w_uq, mla_w_ukv, w_branch, w_out, ffn_w_up, ffn_conv_w, ffn_w_down, pool_w)
    vec = lambda a: a[:, None, :]
    p = dict(norm1_g=vec(norm1_g), norm2_g=vec(norm2_g), mla_kv_norm=vec(mla_kv_norm), mla_q_norm=vec(mla_q_norm),
             rope_place=_rope_placement(), conv_w=conv_w, conv_b=vec(conv_b), conv_ln_g=vec(conv_ln_g),
             conv_ln_b=vec(conv_ln_b), pool_scale=vec(pool_scale))
    tabs_x = _rope_tables(seq)
    tabs_c = _identity_tables(min(tm_c, 512))
    sink_x = _sink_rows(gqa_sink, tq_x)
    sink_c = _sink_rows(gqa_sink, tq_c)

    xs = x.reshape(tx, dm)
    cx = ctx.reshape(tc, dm)
    cond_x = (seq, 0)
    cond_c = (tc, nb)
    for l in range(DEPTH):
        last = l == DEPTH - 1
        zs_c, q_c, k_c, v_c, gq_c, gk_c = _stream_front(cx, mod, cond_c, sw, p, tabs_c, 1, l, tm=tm_c)
        zs_x, q_x, k_x, v_x, gq_x, gk_x = _stream_front(xs, mod, cond_x, sw, p, tabs_x, seq // min(tm_x, 512), l,
                                                        tm=tm_x)
        o_mla = _mla_attn(q_x, k_c, v_c, k_x, v_x, nb=nb, tq=min(256, seq))
        o_gqa = _gqa_attn(gq_x, gk_c, zs_c, gk_x, zs_x, sink_x, l, nb=nb, tq=tq_x)
        xs = _stream_back(xs, mod, cond_x, sw, p, zs_x, o_mla, o_gqa, l,
                          final_norm_g[None] if last else None, tm=tm_x, seq=seq)
        if not last:
            o_mla_c = _mla_attn(q_c, k_c, v_c, None, None, nb=nb, tq=min(256, lc))
            o_gqa_c = _gqa_attn(gq_c, gk_c, zs_c, None, None, sink_c, l, nb=nb, tq=tq_c)
            cx = _stream_back(cx, mod, cond_c, sw, p, zs_c, o_mla_c, o_gqa_c, l, None, tm=tm_c, seq=lc)

    return xs.reshape(nb, seq, dm)
```

```python
import functools

import jax
import jax.numpy as jnp
from jax import lax
from jax.experimental import pallas as pl
from jax.experimental.pallas import tpu as pltpu

F32 = jnp.float32
BF16 = jnp.bfloat16

D_MODEL = 2048
DEPTH = 4
GRID_W = 64
EPS = 1e-6
ROPE_THETA = 10000.0
ROPE_DIM = 64
MLA_HEADS = 4
MLA_NOPE = 128
MLA_V = 128
MLA_RANK = 512
MLA_QK = MLA_NOPE + ROPE_DIM
GQA_HEADS = 8
GQA_KV_HEADS = 2
GQA_GROUP = GQA_HEADS // GQA_KV_HEADS
GQA_HD = 64
WINDOW = 128
CONV_CH = 512
CONV_K = 31
POOL_WINDOWS = (2, 4, 8, 16)
POOL_GC = 128
POOL_CH = 512
N_BRANCH = 4
BRANCH_W = 512
D_FF = 5504
N_MOD = 6

LANES = 128
VMEM_LIMIT_BYTES = 56 * 1024 * 1024

ZS_CONV = 0
ZS_POOL = 1024
ZS_CKV = 1536
ZS_CQ = 2048
ZS_GQ = 2560
ZS_GK = 3072
ZS_GV = 3200
ZS_KR = 3328
ZS_COLS = 3456
MLA_SLOT = 256
FF_PAD = 5632
HALO = 16
CONV_PAD = 16
POOL_REACH = 8
MIX_ROWS = 64
GQA_ROWS = 256
GQA_UNROLL = 4

_SH1, _SC1, _GT1, _SH2, _SC2, _GT2 = range(N_MOD)


def _cparams(*sem):
    return pltpu.CompilerParams(dimension_semantics=sem, vmem_limit_bytes=VMEM_LIMIT_BYTES)


def _dot(a, b):
    return jnp.dot(a, b, preferred_element_type=F32)


def _dot_nt(a, b):
    return lax.dot_general(a, b, (((1,), (1,)), ((), ())), preferred_element_type=F32)


def _sigmoid(v):
    return 1.0 / (1.0 + jnp.exp(-v))


def _rms(v, g):
    return v * lax.rsqrt(jnp.mean(v * v, axis=-1, keepdims=True) + EPS) * g


def _layer_spec(arr, layer):
    zeros = (0,) * (arr.ndim - 1)
    return pl.BlockSpec((1,) + arr.shape[1:], lambda *_: (layer,) + zeros)


def _mod_spec(layer, cond, tm, dm):
    per, base = cond
    return pl.BlockSpec((1, 1, 8, dm), lambda i, *_: (layer, base + (i * tm) // per, 0, 0))


def _ada_kernel(c_ref, w_ref, b_ref, o_ref):
    c = c_ref[...]
    s = (c * _sigmoid(c)).astype(BF16)
    o_ref[0] = _dot(s, w_ref[0].astype(BF16)) + b_ref[0]


def _ada_mod(cs, w_ada, b_ada):
    depth, dm, n = w_ada.shape
    tn = 1024
    return pl.pallas_call(
        _ada_kernel,
        grid=(depth, n // tn),
        in_specs=[pl.BlockSpec((cs.shape[0], dm), lambda l, j: (0, 0)),
                  pl.BlockSpec((1, dm, tn), lambda l, j: (l, 0, j)),
                  pl.BlockSpec((1, 1, tn), lambda l, j: (l, 0, j))],
        out_specs=pl.BlockSpec((1, cs.shape[0], tn), lambda l, j: (l, 0, j)),
        out_shape=jax.ShapeDtypeStruct((depth, cs.shape[0], n), F32),
        compiler_params=_cparams("parallel", "parallel"),
        name="ada_mod",
    )(cs, w_ada, b_ada.reshape(depth, 1, n))


def _nm_matmul_kernel(x_ref, g_ref, mod_ref, w_ref, o_ref, h_ref, *, sh_row, sc_row, sigmoid):
    @pl.when(pl.program_id(1) == 0)
    def _():
        y = _rms(x_ref[...], g_ref[0])
        h = y * (1.0 + mod_ref[0, 0, sc_row:sc_row + 1, :]) + mod_ref[0, 0, sh_row:sh_row + 1, :]
        h_ref[...] = h.astype(BF16)

    acc = _dot(h_ref[...], w_ref[0])
    if sigmoid:
        acc = _sigmoid(acc)
    o_ref[...] = acc.astype(o_ref.dtype)


def _nm_matmul(x, gain, mod, cond, w, layer, *, sh_row, sc_row, sigmoid=False, tm, tn, name):
    t, dm = x.shape
    n = w.shape[2]
    assert t % tm == 0 and n % tn == 0
    return pl.pallas_call(
        functools.partial(_nm_matmul_kernel, sh_row=sh_row, sc_row=sc_row, sigmoid=sigmoid),
        grid=(t // tm, n // tn),
        in_specs=[pl.BlockSpec((tm, dm), lambda i, j: (i, 0)),
                  _layer_spec(gain, layer),
                  _mod_spec(layer, cond, tm, dm),
                  pl.BlockSpec((1, dm, tn), lambda i, j: (layer, 0, j))],
        out_specs=pl.BlockSpec((tm, tn), lambda i, j: (i, j)),
        out_shape=jax.ShapeDtypeStruct((t, n), BF16),
        scratch_shapes=[pltpu.VMEM((tm, dm), BF16)],
        compiler_params=_cparams("parallel", "arbitrary"),
        name=name,
    )(x, gain, mod, w)


def _rope(v, cos, sa, sb):
    return v * cos + pltpu.roll(v, LANES - 16, 1) * sa + pltpu.roll(v, 16, 1) * sb


def _prep_kernel(ckv_ref, cq_ref, gq_ref, tail_ref, cos_ref, sa_ref, sb_ref, kvn_ref, qn_ref,
                 wk_ref, wv_ref, wq_ref, e_ref, q_out, k_out, v_out, gq_out, gk_out):
    cos, sa, sb = cos_ref[...], sa_ref[...], sb_ref[...]
    nkv = _rms(ckv_ref[...].astype(F32), kvn_ref[0]).astype(BF16)
    kr = _rope(tail_ref[:, 2 * LANES:3 * LANES].astype(F32), cos, sa, sb).astype(BF16)
    k_out[...] = (_dot(nkv, wk_ref[0]) + _dot(kr, e_ref[...])).astype(BF16)
    v_out[...] = _dot(nkv, wv_ref[0]).astype(BF16)

    nq = _rms(cq_ref[...].astype(F32), qn_ref[0]).astype(BF16)
    q = _dot(nq, wq_ref[0])
    scale = MLA_QK ** -0.5
    for h in range(MLA_HEADS):
        lo = h * MLA_SLOT
        q_out[:, lo:lo + LANES] = (q[:, lo:lo + LANES] * scale).astype(BF16)
        q_out[:, lo + LANES:lo + 2 * LANES] = (_rope(q[:, lo + LANES:lo + 2 * LANES], cos, sa, sb) * scale).astype(BF16)

    for c in range(GQA_HEADS * GQA_HD // LANES):
        sl = slice(c * LANES, (c + 1) * LANES)
        gq_out[:, sl] = (_rope(gq_ref[:, sl].astype(F32), cos, sa, sb) * GQA_HD ** -0.5).astype(BF16)
    gk_out[...] = _rope(tail_ref[:, 0:LANES].astype(F32), cos, sa, sb).astype(BF16)


def _prep(zs, tabs, tab_period, kvn, qn, wk, wv, wq, e, layer, *, tm):
    t = zs.shape[0]
    assert t % tm == 0
    cb = lambda off, w: pl.BlockSpec((tm, w), lambda i: (i, off // w))
    tab = pl.BlockSpec((tm, LANES), lambda i: (i % tab_period, 0))
    lay = lambda a: _layer_spec(a, layer)
    row = lambda w: pl.BlockSpec((tm, w), lambda i: (i, 0))
    widths = (MLA_HEADS * MLA_SLOT, MLA_HEADS * MLA_SLOT, MLA_HEADS * MLA_V, GQA_HEADS * GQA_HD, GQA_KV_HEADS * GQA_HD)
    return pl.pallas_call(
        _prep_kernel,
        grid=(t // tm,),
        in_specs=[cb(ZS_CKV, MLA_RANK), cb(ZS_CQ, MLA_RANK), cb(ZS_GQ, GQA_HEADS * GQA_HD), cb(ZS_GK, 3 * LANES),
                  tab, tab, tab, lay(kvn), lay(qn), lay(wk), lay(wv), lay(wq),
                  pl.BlockSpec(e.shape, lambda i: (0, 0))],
        out_specs=[row(w) for w in widths],
        out_shape=[jax.ShapeDtypeStruct((t, w), BF16) for w in widths],
        compiler_params=_cparams("parallel"),
        name="qkv_prep",
    )(zs, zs, zs, zs, *tabs, kvn, qn, wk, wv, wq, e)


def _mla_kernel(*refs, has_x):
    if has_x:
        q_ref, kc_ref, vc_ref, kx_ref, vx_ref, o_ref = refs
    else:
        q_ref, kc_ref, vc_ref, o_ref = refs
    for h in range(MLA_HEADS):
        ks = slice(h * MLA_SLOT, (h + 1) * MLA_SLOT)
        vs = slice(h * MLA_V, (h + 1) * MLA_V)
        qh = q_ref[:, ks]
        sc = _dot_nt(qh, kc_ref[:, ks])
        m = jnp.max(sc, axis=-1, keepdims=True)
        if has_x:
            sx = _dot_nt(qh, kx_ref[:, ks])
            m = jnp.maximum(m, jnp.max(sx, axis=-1, keepdims=True))
        pc = jnp.exp(sc - m)
        den = jnp.sum(pc, axis=-1, keepdims=True)
        o = _dot(pc.astype(BF16), vc_ref[:, vs])
        if has_x:
            px = jnp.exp(sx - m)
            den = den + jnp.sum(px, axis=-1, keepdims=True)
            o = o + _dot(px.astype(BF16), vx_ref[:, vs])
        o_ref[:, vs] = (o / den).astype(o_ref.dtype)


def _mla_attn(q, kc, vc, kx, vx, *, nb, tq):
    has_x = kx is not None
    lq = q.shape[0] // nb
    lc = kc.shape[0] // nb
    nq = lq // tq
    kw, vw = MLA_HEADS * MLA_SLOT, MLA_HEADS * MLA_V
    in_specs = [pl.BlockSpec((tq, kw), lambda b, i: (b * nq + i, 0)),
                pl.BlockSpec((lc, kw), lambda b, i: (b, 0)),
                pl.BlockSpec((lc, vw), lambda b, i: (b, 0))]
    args = [q, kc, vc]
    if has_x:
        in_specs += [pl.BlockSpec((lq, kw), lambda b, i: (b, 0)), pl.BlockSpec((lq, vw), lambda b, i: (b, 0))]
        args += [kx, vx]
    return pl.pallas_call(
        functools.partial(_mla_kernel, has_x=has_x),
        grid=(nb, nq),
        in_specs=in_specs,
        out_specs=pl.BlockSpec((tq, vw), lambda b, i: (b * nq + i, 0)),
        out_shape=jax.ShapeDtypeStruct((q.shape[0], vw), BF16),
        compiler_params=_cparams("parallel", "parallel"),
        name="mla_attn_x" if has_x else "mla_attn_c",
    )(*args)


def _gqa_kernel(*refs, has_x, lx, tq):
    if has_x:
        q_ref, kc_ref, vc_ref, kx_ref, vx_ref, sink_ref, o_ref = refs
    else:
        q_ref, kc_ref, vc_ref, sink_ref, o_ref = refs
    rows = GQA_GROUP * tq
    lower = lax.broadcasted_iota(jnp.int32, (1, LANES), 1) < GQA_HD
    slab = 3 * WINDOW

    def chunk(c, carry):
        r0 = pl.multiple_of(c * GQA_ROWS, GQA_ROWS)
        q2 = q_ref[pl.ds(r0, GQA_ROWS), :]
        zero = jnp.zeros_like(q2)
        if has_x:
            t0 = pl.program_id(1) * tq + r0 // GQA_GROUP
            start = pl.multiple_of(jnp.clip((t0 - WINDOW) // WINDOW * WINDOW, 0, lx - slab), LANES)
            kslab = kx_ref[pl.ds(start, slab), :]
            vslab = vx_ref[pl.ds(start, slab), :]
            qpos = t0 + lax.broadcasted_iota(jnp.int32, (GQA_ROWS, slab), 0) // GQA_GROUP
            kpos = start + lax.broadcasted_iota(jnp.int32, (GQA_ROWS, slab), 1)
            valid = jnp.abs(qpos - kpos) <= WINDOW
        outs = []
        for half in range(GQA_KV_HEADS):
            qh = jnp.where(lower, q2, zero) if half == 0 else jnp.where(lower, zero, q2)
            sink = sink_ref[0, pl.ds(r0, GQA_ROWS), half:half + 1]
            sc = _dot_nt(qh, kc_ref[...])
            m = jnp.maximum(jnp.max(sc, axis=-1, keepdims=True), sink)
            if has_x:
                sw = jnp.where(valid, _dot_nt(qh, kslab), -1e30)
                m = jnp.maximum(m, jnp.max(sw, axis=-1, keepdims=True))
            pc = jnp.exp(sc - m)
            den = jnp.sum(pc, axis=-1, keepdims=True) + jnp.exp(sink - m)
            o = _dot(pc.astype(BF16), vc_ref[...])
            if has_x:
                pw = jnp.exp(sw - m)
                den = den + jnp.sum(pw, axis=-1, keepdims=True)
                o = o + _dot(pw.astype(BF16), vslab)
            outs.append(o / den)
        o_ref[pl.ds(r0, GQA_ROWS), :] = jnp.where(lower, outs[0], outs[1]).astype(o_ref.dtype)
        return carry

    lax.fori_loop(0, rows // GQA_ROWS, chunk, 0, unroll=GQA_UNROLL)


def _gqa_attn(gq, gkc, zsc, gkx, zsx, sink_rows, layer, *, nb, tq):
    has_x = gkx is not None
    lq = gq.shape[0] // nb
    lc = gkc.shape[0] // nb
    nq = lq // tq
    kw = GQA_KV_HEADS * GQA_HD
    vcol = ZS_GV // kw
    rows = GQA_GROUP * tq
    q2 = gq.reshape(gq.shape[0] * GQA_GROUP, kw)
    in_specs = [pl.BlockSpec((rows, kw), lambda b, i: (b * nq + i, 0)),
                pl.BlockSpec((lc, kw), lambda b, i: (b, 0)),
                pl.BlockSpec((lc, kw), lambda b, i: (b, vcol))]
    args = [q2, gkc, zsc]
    if has_x:
        assert lq >= 3 * WINDOW and tq % (GQA_ROWS // GQA_GROUP) == 0 and WINDOW % (GQA_ROWS // GQA_GROUP) == 0
        in_specs +=[pl.BlockSpec((lq, kw), lambda b, i: (b, 0)), pl.BlockSpec((lq, kw), lambda b, i: (b, vcol))]
        args += [gkx, zsx]
    assert sink_rows.shape[1] == rows
    in_specs.append(_layer_spec(sink_rows, layer))
    args.append(sink_rows)
    out = pl.pallas_call(
        functools.partial(_gqa_kernel, has_x=has_x, lx=lq, tq=tq),
        grid=(nb, nq),
        in_specs=in_specs,
        out_specs=pl.BlockSpec((rows, kw), lambda b, i: (b * nq + i, 0)),
        out_shape=jax.ShapeDtypeStruct(q2.shape, BF16),
        compiler_params=_cparams("parallel", "parallel"),
        name="gqa_attn_x" if has_x else "gqa_attn_c",
    )(*args)
    return out.reshape(gq.shape)


def _mixers_kernel(z_ref, cw_ref, cb_ref, lg_ref, lb_ref, pw_ref, ps_ref, oc_ref, op_ref, pa_ref, pp_ref, *, seq):
    zpad = jnp.zeros((CONV_PAD, CONV_CH), F32)
    a = z_ref[:, 0:CONV_CH].astype(F32)
    g = z_ref[:, CONV_CH:2 * CONV_CH].astype(F32)
    pa_ref[0:CONV_PAD, :] = zpad
    pa_ref[CONV_PAD + seq:2 * CONV_PAD + seq, :] = zpad
    pa_ref[CONV_PAD:CONV_PAD + seq, :] = a * _sigmoid(g)
    pp_ref[0:CONV_PAD, :] = zpad
    pp_ref[CONV_PAD + seq:2 * CONV_PAD + seq, :] = zpad
    pp_ref[CONV_PAD:CONV_PAD + seq, :] = z_ref[:, 2 * CONV_CH:2 * CONV_CH + POOL_CH].astype(F32)

    def shifted(win, j):
        return win if j == 0 else pltpu.roll(win, win.shape[0] - j, 0)

    def step(c, carry):
        r0 = pl.multiple_of(c * MIX_ROWS, MIX_ROWS)
        win = pa_ref[pl.ds(r0, MIX_ROWS + 2 * CONV_PAD), :]
        acc = jnp.zeros((MIX_ROWS, CONV_CH), F32)
        for j in range(8):
            wj = shifted(win, j)
            for q in range(2 * CONV_PAD // 8):
                k = 8 * q + j - (CONV_PAD - CONV_K // 2)
                if 0 <= k < CONV_K:
                    acc = acc + cw_ref[0, k:k + 1, :] * wj[8 * q:8 * q + MIX_ROWS, :]
        u = acc + cb_ref[0]
        mu = jnp.mean(u, axis=-1, keepdims=True)
        d = u - mu
        y = d * lax.rsqrt(jnp.mean(d * d, axis=-1, keepdims=True) + EPS) * lg_ref[0] + lb_ref[0]
        oc_ref[pl.ds(r0, MIX_ROWS), :] = (y * _sigmoid(y)).astype(BF16)
        pwin = pp_ref[pl.ds(r0 + (CONV_PAD - POOL_REACH), MIX_ROWS + 2 * POOL_REACH), :]
        sums = [jnp.zeros((MIX_ROWS, POOL_GC), F32) for _ in POOL_WINDOWS]
        for j in range(8):
            wj = shifted(pwin, j)
            for q in range(2 * POOL_REACH // 8):
                dlt = 8 * q + j - POOL_REACH
                for gi, w in enumerate(POOL_WINDOWS):
                    if -(w // 2) <= dlt < w - w // 2:
                        sums[gi] = sums[gi] + wj[8 * q:8 * q + MIX_ROWS, gi * POOL_GC:(gi + 1) * POOL_GC]
        t = r0 + lax.broadcasted_iota(jnp.int32, (MIX_ROWS, 1), 0)
        for gi, w in enumerate(POOL_WINDOWS):
            ls = slice(gi * POOL_GC, (gi + 1) * POOL_GC)
            cnt = (jnp.minimum(t - w // 2 + w, seq) - jnp.maximum(t - w // 2, 0)).astype(F32)
            p = sums[gi] / cnt - pwin[POOL_REACH:POOL_REACH + MIX_ROWS, ls]
            op_ref[pl.ds(r0, MIX_ROWS), ls] = (_dot(p.astype(BF16), pw_ref[0, gi]) * ps_ref[0][:, ls]).astype(BF16)
        return carry

    lax.fori_loop(0, seq // MIX_ROWS, step, 0)


def _mixers(zs, cw, cb, lg, lb, pw, ps, layer, *, seq):
    t = zs.shape[0]
    assert t % seq == 0 and seq % MIX_ROWS == 0
    lay = lambda a: _layer_spec(a, layer)
    zw = 2 * CONV_CH + POOL_CH
    out = pl.BlockSpec((seq, CONV_CH), lambda s: (s, 0))
    return pl.pallas_call(
        functools.partial(_mixers_kernel, seq=seq),
        grid=(t // seq,),
        in_specs=[pl.BlockSpec((seq, zw), lambda s: (s, 0)), lay(cw), lay(cb), lay(lg), lay(lb), lay(pw), lay(ps)],
        out_specs=[out, out],
        out_shape=[jax.ShapeDtypeStruct((t, CONV_CH), BF16), jax.ShapeDtypeStruct((t, POOL_CH), BF16)],
        scratch_shapes=[pltpu.VMEM((seq + 2 * CONV_PAD, CONV_CH), F32), pltpu.VMEM((seq + 2 * CONV_PAD, POOL_CH), F32)],
        compiler_params=_cparams("parallel"),
        name="local_mixers",
    )(zs, cw, cb, lg, lb, pw, ps)


def _merge_kernel(oc_ref, om_ref, og_ref, op_ref, sg_ref, wb_ref, y_ref):
    acc = None
    for n, br in enumerate((oc_ref, om_ref, og_ref, op_ref)):
        term = sg_ref[:, n * D_MODEL:(n + 1) * D_MODEL].astype(F32) * _dot(br[...], wb_ref[0, n])
        acc = term if acc is None else acc + term
    y_ref[...] = acc.astype(BF16)


def _merge(branches, sg, wb, layer, *, tm):
    t = sg.shape[0]
    assert t % tm == 0
    br = pl.BlockSpec((tm, BRANCH_W), lambda i: (i, 0))
    return pl.pallas_call(
        _merge_kernel,
        grid=(t // tm,),
        in_specs=[br, br, br, br, pl.BlockSpec((tm, N_BRANCH * D_MODEL), lambda i: (i, 0)), _layer_spec(wb, layer)],
        out_specs=pl.BlockSpec((tm, D_MODEL), lambda i: (i, 0)),
        out_shape=jax.ShapeDtypeStruct((t, D_MODEL), BF16),
        compiler_params=_cparams("parallel"),
        name="gated_merge",
    )(*branches, sg, wb)


def _resid_matmul_kernel(y_ref, w_ref, x_ref, mod_ref, o_ref, *, gt_row):
    o_ref[...] = x_ref[...] + mod_ref[0, 0, gt_row:gt_row + 1, :] * _dot(y_ref[...], w_ref[0])


def _resid_matmul(y, w, x, mod, cond, layer, *, gt_row, tm):
    t, dm = x.shape
    assert t % tm == 0
    return pl.pallas_call(
        functools.partial(_resid_matmul_kernel, gt_row=gt_row),
        grid=(t // tm,),
        in_specs=[pl.BlockSpec((tm, y.shape[1]), lambda i: (i, 0)),
                  _layer_spec(w, layer),
                  pl.BlockSpec((tm, dm), lambda i: (i, 0)),
                  _mod_spec(layer, cond, tm, dm)],
        out_specs=pl.BlockSpec((tm, dm), lambda i: (i, 0)),
        out_shape=jax.ShapeDtypeStruct((t, dm), F32),
        compiler_params=_cparams("parallel"),
        name="out_proj",
    )(y, w, x, mod)


def _ffn_kernel(x_ref, xt_ref, xb_ref, g_ref, mod_ref, wa_ref, wg_ref, cwa_ref, cwg_ref, wd_ref, *rest, seq, final):
    if final:
        fg_ref, o_ref, h_ref = rest
    else:
        o_ref, h_ref = rest
    i, k = pl.program_id(0), pl.program_id(1)
    tm = x_ref.shape[0]
    half = HALO // 2
    multi = tm > seq

    def normmod(x):
        y = _rms(x, g_ref[0])
        return (y * (1.0 + mod_ref[0, 0, _SC2:_SC2 + 1, :]) + mod_ref[0, 0, _SH2:_SH2 + 1, :]).astype(BF16)

    @pl.when(k == 0)
    def _():
        h_ref[HALO:HALO + tm, :] = normmod(x_ref[...])
        nb = normmod(jnp.concatenate([xb_ref[...], xt_ref[...]], axis=0))
        if not multi:
            from_next = lax.broadcasted_iota(jnp.int32, (HALO, 1), 0) < half
            boundary = jnp.where(from_next, ((i + 1) * tm) % seq, (i * tm) % seq)
            nb = jnp.where(boundary == 0, jnp.zeros_like(nb), nb)
        h_ref[0:HALO, :] = nb
        o_ref[...] = x_ref[...]

    if multi:
        pos = lax.broadcasted_iota(jnp.int32, (tm, 1), 0) % seq
        first, last = pos == 0, pos == seq - 1

    def conv(u, cw_ref):
        cur = u[HALO:HALO + tm]
        prev = pltpu.roll(u, 1, 0)[HALO:HALO + tm]
        nxt = pltpu.roll(u, tm + HALO - 1, 0)[HALO:HALO + tm]
        if multi:
            prev = jnp.where(first, 0.0, prev)
            nxt = jnp.where(last, 0.0, nxt)
        return cw_ref[0, 0:1, :] * prev + cw_ref[0, 1:2, :] * cur + cw_ref[0, 2:3, :] * nxt

    h = h_ref[...]
    a = conv(_dot(h, wa_ref[0]), cwa_ref)
    g = conv(_dot(h, wg_ref[0]), cwg_ref)
    act = (g * _sigmoid(g) * a).astype(BF16)
    o_ref[...] += mod_ref[0, 0, _GT2:_GT2 + 1, :] * _dot(act, wd_ref[0])

    if final:
        @pl.when(k == pl.num_programs(1) - 1)
        def _():
            o_ref[...] = _rms(o_ref[...], fg_ref[...])


def _ffn(x, gain, mod, cond, w_up, cw, w_down, layer, final_gain, *, seq, tm, tf):
    t, dm = x.shape
    fp = w_down.shape[1]
    half = HALO // 2
    assert t % tm == 0 and fp % tf == 0 and tm % HALO == 0 and t % seq == 0
    assert seq % tm == 0 or tm % seq == 0
    nk = fp // tf
    hb = tm // half
    last_hb = t // half - 1
    final = final_gain is not None
    in_specs = [pl.BlockSpec((tm, dm), lambda i, k: (i, 0)),
                pl.BlockSpec((half, dm), lambda i, k: (jnp.maximum(i * hb - 1, 0), 0)),
                pl.BlockSpec((half, dm), lambda i, k: (jnp.minimum((i + 1) * hb, last_hb), 0)),
                _layer_spec(gain, layer),
                _mod_spec(layer, cond, tm, dm),
                pl.BlockSpec((1, dm, tf), lambda i, k: (layer, 0, k)),
                pl.BlockSpec((1, dm, tf), lambda i, k: (layer, 0, nk + k)),
                pl.BlockSpec((1, 8, tf), lambda i, k: (layer, 0, k)),
                pl.BlockSpec((1, 8, tf), lambda i, k: (layer, 0, nk + k)),
                pl.BlockSpec((1, tf, dm), lambda i, k: (layer, k, 0))]
    args = [x, x, x, gain, mod, w_up, w_up, cw, cw, w_down]
    if final:
        in_specs.append(pl.BlockSpec((1, dm), lambda i, k: (0, 0)))
        args.append(final_gain)
    return pl.pallas_call(
        functools.partial(_ffn_kernel, seq=seq, final=final),
        grid=(t // tm, nk),
        in_specs=in_specs,
        out_specs=pl.BlockSpec((tm, dm), lambda i, k: (i, 0)),
        out_shape=jax.ShapeDtypeStruct((t, dm), F32),
        scratch_shapes=[pltpu.VMEM((tm + HALO, dm), BF16)],
        compiler_params=_cparams("parallel", "arbitrary"),
        name="conv_ffn",
    )(*args)


def _rope_tables(seq):
    t = jnp.arange(seq)
    row = (t // GRID_W).astype(F32)
    col = (t % GRID_W).astype(F32)
    half = ROPE_DIM // 2
    inv_freq = ROPE_THETA ** (-jnp.arange(0, half, 2, dtype=F32) / half)
    d = jnp.arange(ROPE_DIM)
    ang = jnp.where((d < half)[None, :], row[:, None], col[:, None]) * inv_freq[d % (half // 2)][None, :]
    upper = ((d % half) < half // 2)[None, :]
    cos = jnp.cos(ang)
    sin = jnp.sin(ang)
    sa = jnp.where(upper, -sin, 0.0)
    sb = jnp.where(upper, 0.0, sin)
    rep = lambda a: jnp.tile(a, (1, LANES // ROPE_DIM)).astype(F32)
    return rep(cos), rep(sa), rep(sb)


def _identity_tables(rows):
    return jnp.ones((rows, LANES), F32), jnp.zeros((rows, LANES), F32), jnp.zeros((rows, LANES), F32)


def _group_major(a, axis):
    order = [kv * GQA_GROUP + g for g in range(GQA_GROUP) for kv in range(GQA_KV_HEADS)]
    take = lambda h: lax.slice_in_dim(a, h * GQA_HD, (h + 1) * GQA_HD, axis=axis)
    return jnp.concatenate([take(h) for h in order], axis=axis)


def _stacked_weights(w_in, mla_w_uq, mla_w_ukv, w_branch, w_out, ffn_w_up, ffn_conv_w, ffn_w_down, pool_w):
    depth = w_in.shape[0]
    c = lambda a, b: w_in[:, :, a:b].astype(BF16)
    zeros = lambda arr, n: jnp.zeros(arr.shape[:-1] + (n,), BF16)
    w_small = jnp.concatenate([c(1856, 2880), c(2880, 3392), c(0, 512), c(832, 1344), _group_major(c(1344, 1856), 2),
                               c(576, 704), c(704, 832), c(512, 576), zeros(w_in, ZS_COLS - ZS_KR - ROPE_DIM)], axis=2)
    w_gate = c(3392, 3392 + N_BRANCH * D_MODEL)
    uq, ukv = mla_w_uq.astype(BF16), mla_w_ukv.astype(BF16)
    wq, wk, wv = [], [], []
    for h in range(MLA_HEADS):
        wq += [uq[:, :, h * MLA_QK:(h + 1) * MLA_QK], zeros(uq, MLA_SLOT - MLA_QK)]
        lo = h * (MLA_NOPE + MLA_V)
        wk += [ukv[:, :, lo:lo + MLA_NOPE], zeros(ukv, MLA_SLOT - MLA_NOPE)]
        wv.append(ukv[:, :, lo + MLA_NOPE:lo + MLA_NOPE + MLA_V])
    wq, wk, wv = (jnp.concatenate(parts, axis=2) for parts in (wq, wk, wv))
    wb = w_branch.astype(BF16)
    wb = jnp.concatenate([wb[:, :2], _group_major(wb[:, 2], 1)[:, None], wb[:, 3:]], axis=1)
    padc = FF_PAD - D_FF
    up = ffn_w_up.astype(BF16)
    w_up = jnp.concatenate([up[:, :, :D_FF], zeros(up, padc), up[:, :, D_FF:], zeros(up, padc)], axis=2)
    cwz = jnp.zeros(ffn_conv_w.shape[:-1] + (padc,), ffn_conv_w.dtype)
    cw = jnp.concatenate([ffn_conv_w[:, :, :D_FF], cwz, ffn_conv_w[:, :, D_FF:], cwz], axis=2)
    cw = jnp.pad(cw, ((0, 0), (0, 8 - cw.shape[1]), (0, 0)))
    w_down = jnp.pad(ffn_w_down.astype(BF16), ((0, 0), (0, padc), (0, 0)))
    return dict(w_small=w_small, w_gate=w_gate, wq=wq, wk=wk, wv=wv, w_branch=wb, w_out=w_out.astype(BF16),
                w_up=w_up, ffn_cw=cw, w_down=w_down, pool_w=pool_w.astype(BF16))


def _rope_placement():
    d = jnp.arange(LANES)[:, None]
    col = jnp.arange(MLA_HEADS * MLA_SLOT)[None, :]
    return ((col % MLA_SLOT == MLA_NOPE + d) & (d < ROPE_DIM)).astype(BF16)


def _sink_rows(gqa_sink, tq):
    depth = gqa_sink.shape[0]
    s = gqa_sink.reshape(depth, GQA_KV_HEADS, GQA_GROUP).transpose(0, 2, 1)
    s = jnp.pad(s, ((0, 0), (0, 0), (0, LANES - GQA_KV_HEADS)))
    return jnp.tile(s, (1, tq, 1))


def _stream_front(x, mod, cond, sw, p, tabs, tab_period, layer, *, tm):
    zs = _nm_matmul(x, p["norm1_g"], mod, cond, sw["w_small"], layer, sh_row=_SH1, sc_row=_SC1,
                    tm=tm, tn=ZS_COLS // 3, name="in_proj_small")
    q, k, v, gq, gk = _prep(zs, tabs, tab_period, p["mla_kv_norm"], p["mla_q_norm"], sw["wk"], sw["wv"], sw["wq"],
                            p["rope_place"], layer, tm=min(tm, 512))
    return zs, q, k, v, gq, gk


def _stream_back(x, mod, cond, sw, p, zs, o_mla, o_gqa, layer, final_gain, *, tm, seq):
    o_conv, o_pool = _mixers(zs, p["conv_w"], p["conv_b"], p["conv_ln_g"], p["conv_ln_b"], sw["pool_w"],
                             p["pool_scale"], layer, seq=seq)
    sg = _nm_matmul(x, p["norm1_g"], mod, cond, sw["w_gate"], layer, sh_row=_SH1, sc_row=_SC1, sigmoid=True,
                    tm=tm, tn=1024, name="in_proj_gates")
    y = _merge((o_conv, o_mla, o_gqa, o_pool), sg, sw["w_branch"], layer, tm=min(tm, 256))
    x = _resid_matmul(y, sw["w_out"], x, mod, cond, layer, gt_row=_GT1, tm=min(tm, 512))
    return _ffn(x, p["norm2_g"], mod, cond, sw["w_up"], sw["ffn_cw"], sw["w_down"], layer, final_gain,
                seq=seq, tm=min(tm, 512), tf=512)


def kernel(x, c, ctx, c_ctx, norm1_g, norm2_g, w_ada, b_ada, w_in, mla_q_norm, mla_w_uq, mla_kv_norm, mla_w_ukv,
           gqa_sink, conv_w, conv_b, conv_ln_g, conv_ln_b, pool_w, pool_scale, w_branch, w_out, ffn_w_up,
           ffn_conv_w, ffn_w_down, final_norm_g):
    nb, seq, dm = x.shape
    lc = ctx.shape[1]
    tx, tc = nb * seq, nb * lc
    tm_x = min(1024, seq)
    tm_c = min(1024, tc)
    tq_x = min(256, seq - 2 * WINDOW)
    tq_c = min(256, lc)

    cs = jnp.zeros((16, dm), F32).at[:nb].set(c).at[nb].set(c_ctx)
    mod = _ada_mod(cs, w_ada, b_ada).reshape(DEPTH, 16, N_MOD, dm)
    mod = jnp.pad(mod, ((0, 0), (0, 0), (0, 8 - N_MOD), (0, 0)))

    sw = _stacked_weights(w_in, mla_w_uq, mla_w_ukv, w_branch, w_out, ffn_w_up, ffn_conv_w, ffn_w_down, pool_w)
    vec = lambda a: a[:, None, :]
    p = dict(norm1_g=vec(norm1_g), norm2_g=vec(norm2_g), mla_kv_norm=vec(mla_kv_norm), mla_q_norm=vec(mla_q_norm),
             rope_place=_rope_placement(), conv_w=conv_w, conv_b=vec(conv_b), conv_ln_g=vec(conv_ln_g),
             conv_ln_b=vec(conv_ln_b), pool_scale=vec(pool_scale))
    tabs_x = _rope_tables(seq)
    tabs_c = _identity_tables(min(tm_c, 512))
    sink_x = _sink_rows(gqa_sink, tq_x)
    sink_c = _sink_rows(gqa_sink, tq_c)

    xs = x.reshape(tx, dm)
    cx = ctx.reshape(tc, dm)
    cond_x = (seq, 0)
    cond_c = (tc, nb)
    for l in range(DEPTH):
        last = l == DEPTH - 1
        zs_c, q_c, k_c, v_c, gq_c, gk_c = _stream_front(cx, mod, cond_c, sw, p, tabs_c, 1, l, tm=tm_c)
        zs_x, q_x, k_x, v_x, gq_x, gk_x = _stream_front(xs, mod, cond_x, sw, p, tabs_x, seq // min(tm_x, 512), l,
                                                        tm=tm_x)
        o_mla = _mla_attn(q_x, k_c, v_c, k_x, v_x, nb=nb, tq=min(512, seq))
        o_gqa = _gqa_attn(gq_x, gk_c, zs_c, gk_x, zs_x, sink_x, l, nb=nb, tq=tq_x)
        xs = _stream_back(xs, mod, cond_x, sw, p, zs_x, o_mla, o_gqa, l,
                          final_norm_g[None] if last else None, tm=tm_x, seq=seq)
        if not last:
            o_mla_c = _mla_attn(q_c, k_c, v_c, None, None, nb=nb, tq=min(256, lc))
            o_gqa_c = _gqa_attn(gq_c, gk_c, zs_c, None, None, sink_c, l, nb=nb, tq=tq_c)
            cx = _stream_back(cx, mod, cond_c, sw, p, zs_c, o_mla_c, o_gqa_c, l, None, tm=tm_c, seq=lc)

    return xs.reshape(nb, seq, dm)
```

```python
import functools

import jax
import jax.numpy as jnp
from jax import lax
from jax.experimental import pallas as pl
from jax.experimental.pallas import tpu as pltpu

F32 = jnp.float32
BF16 = jnp.bfloat16

D_MODEL = 2048
DEPTH = 4
GRID_W = 64
EPS = 1e-6
ROPE_THETA = 10000.0
ROPE_DIM = 64
MLA_HEADS = 4
MLA_NOPE = 128
MLA_V = 128
MLA_RANK = 512
MLA_QK = MLA_NOPE + ROPE_DIM
GQA_HEADS = 8
GQA_KV_HEADS = 2
GQA_GROUP = GQA_HEADS // GQA_KV_HEADS
GQA_HD = 64
WINDOW = 128
CONV_CH = 512
CONV_K = 31
POOL_WINDOWS = (2, 4, 8, 16)
POOL_GC = 128
POOL_CH = 512
N_BRANCH = 4
BRANCH_W = 512
D_FF = 5504
N_MOD = 6

LANES = 128
VMEM_LIMIT_BYTES = 56 * 1024 * 1024

ZS_CONV = 0
ZS_POOL = 1024
ZS_CKV = 1536
ZS_CQ = 2048
ZS_GQ = 2560
ZS_GK = 3072
ZS_GV = 3200
ZS_KR = 3328
ZS_COLS = 3584
ZS_TN = 512
MLA_SLOT = 256
FF_PAD = 5632
HALO = 16
CONV_PAD = 16
POOL_REACH = 8
MIX_ROWS = 64
GQA_ROWS = 256
GQA_UNROLL = 4

_SH1, _SC1, _GT1, _SH2, _SC2, _GT2 = range(N_MOD)


def _cparams(*sem):
    return pltpu.CompilerParams(dimension_semantics=sem, vmem_limit_bytes=VMEM_LIMIT_BYTES)


def _dot(a, b):
    return jnp.dot(a, b, preferred_element_type=F32)


def _dot_nt(a, b):
    return lax.dot_general(a, b, (((1,), (1,)), ((), ())), preferred_element_type=F32)


def _sigmoid(v):
    return 1.0 / (1.0 + jnp.exp(-v))


def _rms(v, g):
    return v * lax.rsqrt(jnp.mean(v * v, axis=-1, keepdims=True) + EPS) * g


def _layer_spec(arr, layer):
    zeros = (0,) * (arr.ndim - 1)
    return pl.BlockSpec((1,) + arr.shape[1:], lambda *_: (layer,) + zeros)


def _mod_spec(layer, cond, tm, dm):
    per, base = cond
    return pl.BlockSpec((1, 1, 8, dm), lambda i, *_: (layer, base + (i * tm) // per, 0, 0))


def _ada_kernel(c_ref, w_ref, b_ref, o_ref):
    c = c_ref[...]
    s = (c * _sigmoid(c)).astype(BF16)
    o_ref[0] = _dot(s, w_ref[0].astype(BF16)) + b_ref[0]


def _ada_mod(cs, w_ada, b_ada):
    depth, dm, n = w_ada.shape
    tn = 1024
    return pl.pallas_call(
        _ada_kernel,
        grid=(depth, n // tn),
        in_specs=[pl.BlockSpec((cs.shape[0], dm), lambda l, j: (0, 0)),
                  pl.BlockSpec((1, dm, tn), lambda l, j: (l, 0, j)),
                  pl.BlockSpec((1, 1, tn), lambda l, j: (l, 0, j))],
        out_specs=pl.BlockSpec((1, cs.shape[0], tn), lambda l, j: (l, 0, j)),
        out_shape=jax.ShapeDtypeStruct((depth, cs.shape[0], n), F32),
        compiler_params=_cparams("parallel", "parallel"),
        name="ada_mod",
    )(cs, w_ada, b_ada.reshape(depth, 1, n))


def _norm1(x_ref, g_ref, mod_ref):
    y = _rms(x_ref[...], g_ref[0])
    return (y * (1.0 + mod_ref[0, 0, _SC1:_SC1 + 1, :]) + mod_ref[0, 0, _SH1:_SH1 + 1, :]).astype(BF16)


def _nm_matmul_kernel(x_ref, g_ref, mod_ref, w_ref, o_ref, h_ref):
    @pl.when(pl.program_id(1) == 0)
    def _():
        h_ref[...] = _norm1(x_ref, g_ref, mod_ref)

    o_ref[...] = _dot(h_ref[...], w_ref[0]).astype(o_ref.dtype)


def _nm_matmul(x, gain, mod, cond, w, layer, *, tm, tn, name):
    t, dm = x.shape
    n = w.shape[2]
    assert t % tm == 0 and n % tn == 0
    return pl.pallas_call(
        _nm_matmul_kernel,
        grid=(t // tm, n // tn),
        in_specs=[pl.BlockSpec((tm, dm), lambda i, j: (i, 0)),
                  _layer_spec(gain, layer),
                  _mod_spec(layer, cond, tm, dm),
                  pl.BlockSpec((1, dm, tn), lambda i, j: (layer, 0, j))],
        out_specs=pl.BlockSpec((tm, tn), lambda i, j: (i, j)),
        out_shape=jax.ShapeDtypeStruct((t, n), BF16),
        scratch_shapes=[pltpu.VMEM((tm, dm), BF16)],
        compiler_params=_cparams("parallel", "arbitrary"),
        name=name,
    )(x, gain, mod, w)


def _in_proj_kernel(x_ref, g_ref, mod_ref, ws_ref, wg_ref, zs_ref, sg_ref, h_ref, *, ns):
    j = pl.program_id(1)

    @pl.when(j == 0)
    def _():
        h_ref[...] = _norm1(x_ref, g_ref, mod_ref)

    @pl.when(j < ns)
    def _():
        zs_ref[...] = _dot(h_ref[...], ws_ref[0]).astype(BF16)

    @pl.when(j >= ns)
    def _():
        sg_ref[...] = _sigmoid(_dot(h_ref[...], wg_ref[0])).astype(BF16)


def _in_proj(x, gain, mod, cond, w_small, w_gate, layer, *, tm, tn_small, tn_gate):
    t, dm = x.shape
    n_small, n_gate = w_small.shape[2], w_gate.shape[2]
    assert t % tm == 0 and n_small % tn_small == 0 and n_gate % tn_gate == 0
    ns, ng = n_small // tn_small, n_gate // tn_gate
    small_blk = lambda j: jnp.minimum(j, ns - 1)
    gate_blk = lambda j: jnp.maximum(j - ns, 0)
    return pl.pallas_call(
        functools.partial(_in_proj_kernel, ns=ns),
        grid=(t // tm, ns + ng),
        in_specs=[pl.BlockSpec((tm, dm), lambda i, j: (i, 0)),
                  _layer_spec(gain, layer),
                  _mod_spec(layer, cond, tm, dm),
                  pl.BlockSpec((1, dm, tn_small), lambda i, j: (layer, 0, small_blk(j))),
                  pl.BlockSpec((1, dm, tn_gate), lambda i, j: (layer, 0, gate_blk(j)))],
        out_specs=[pl.BlockSpec((tm, tn_small), lambda i, j: (i, small_blk(j))),
                   pl.BlockSpec((tm, tn_gate), lambda i, j: (i, gate_blk(j)))],
        out_shape=[jax.ShapeDtypeStruct((t, n_small), BF16), jax.ShapeDtypeStruct((t, n_gate), BF16)],
        scratch_shapes=[pltpu.VMEM((tm, dm), BF16)],
        compiler_params=_cparams("parallel", "arbitrary"),
        name="in_proj",
    )(x, gain, mod, w_small, w_gate)


def _rope(v, cos, sa, sb):
    return v * cos + pltpu.roll(v, LANES - 16, 1) * sa + pltpu.roll(v, 16, 1) * sb


def _prep_kernel(ckv_ref, cq_ref, gq_ref, tail_ref, cos_ref, sa_ref, sb_ref, kvn_ref, qn_ref,
                 wk_ref, wv_ref, wq_ref, e_ref, q_out, k_out, v_out, gq_out, gk_out):
    cos, sa, sb = cos_ref[...], sa_ref[...], sb_ref[...]
    nkv = _rms(ckv_ref[...].astype(F32), kvn_ref[0]).astype(BF16)
    kr = _rope(tail_ref[:, 2 * LANES:3 * LANES].astype(F32), cos, sa, sb).astype(BF16)
    k_out[...] = (_dot(nkv, wk_ref[0]) + _dot(kr, e_ref[...])).astype(BF16)
    v_out[...] = _dot(nkv, wv_ref[0]).astype(BF16)

    nq = _rms(cq_ref[...].astype(F32), qn_ref[0]).astype(BF16)
    q = _dot(nq, wq_ref[0])
    scale = MLA_QK ** -0.5
    for h in range(MLA_HEADS):
        lo = h * MLA_SLOT
        q_out[:, lo:lo + LANES] = (q[:, lo:lo + LANES] * scale).astype(BF16)
        q_out[:, lo + LANES:lo + 2 * LANES] = (_rope(q[:, lo + LANES:lo + 2 * LANES], cos, sa, sb) * scale).astype(BF16)

    for c in range(GQA_HEADS * GQA_HD // LANES):
        sl = slice(c * LANES, (c + 1) * LANES)
        gq_out[:, sl] = (_rope(gq_ref[:, sl].astype(F32), cos, sa, sb) * GQA_HD ** -0.5).astype(BF16)
    gk_out[...] = _rope(tail_ref[:, 0:LANES].astype(F32), cos, sa, sb).astype(BF16)


def _prep(zs, tabs, tab_period, kvn, qn, wk, wv, wq, e, layer, *, tm):
    t = zs.shape[0]
    assert t % tm == 0
    cb = lambda off, w: pl.BlockSpec((tm, w), lambda i: (i, off // w))
    tab = pl.BlockSpec((tm, LANES), lambda i: (i % tab_period, 0))
    lay = lambda a: _layer_spec(a, layer)
    row = lambda w: pl.BlockSpec((tm, w), lambda i: (i, 0))
    widths = (MLA_HEADS * MLA_SLOT, MLA_HEADS * MLA_SLOT, MLA_HEADS * MLA_V, GQA_HEADS * GQA_HD, GQA_KV_HEADS * GQA_HD)
    return pl.pallas_call(
        _prep_kernel,
        grid=(t // tm,),
        in_specs=[cb(ZS_CKV, MLA_RANK), cb(ZS_CQ, MLA_RANK), cb(ZS_GQ, GQA_HEADS * GQA_HD), cb(ZS_GK, 3 * LANES),
                  tab, tab, tab, lay(kvn), lay(qn), lay(wk), lay(wv), lay(wq),
                  pl.BlockSpec(e.shape, lambda i: (0, 0))],
        out_specs=[row(w) for w in widths],
        out_shape=[jax.ShapeDtypeStruct((t, w), BF16) for w in widths],
        compiler_params=_cparams("parallel"),
        name="qkv_prep",
    )(zs, zs, zs, zs, *tabs, kvn, qn, wk, wv, wq, e)


def _mla_kernel(*refs, has_x):
    if has_x:
        q_ref, kc_ref, vc_ref, kx_ref, vx_ref, o_ref = refs
    else:
        q_ref, kc_ref, vc_ref, o_ref = refs
    for h in range(MLA_HEADS):
        ks = slice(h * MLA_SLOT, (h + 1) * MLA_SLOT)
        vs = slice(h * MLA_V, (h + 1) * MLA_V)
        qh = q_ref[:, ks]
        sc = _dot_nt(qh, kc_ref[:, ks])
        m = jnp.max(sc, axis=-1, keepdims=True)
        if has_x:
            sx = _dot_nt(qh, kx_ref[:, ks])
            m = jnp.maximum(m, jnp.max(sx, axis=-1, keepdims=True))
        pc = jnp.exp(sc - m)
        den = jnp.sum(pc, axis=-1, keepdims=True)
        o = _dot(pc.astype(BF16), vc_ref[:, vs])
        if has_x:
            px = jnp.exp(sx - m)
            den = den + jnp.sum(px, axis=-1, keepdims=True)
            o = o + _dot(px.astype(BF16), vx_ref[:, vs])
        o_ref[:, vs] = (o / den).astype(o_ref.dtype)


def _mla_attn(q, kc, vc, kx, vx, *, nb, tq):
    has_x = kx is not None
    lq = q.shape[0] // nb
    lc = kc.shape[0] // nb
    nq = lq // tq
    kw, vw = MLA_HEADS * MLA_SLOT, MLA_HEADS * MLA_V
    in_specs = [pl.BlockSpec((tq, kw), lambda b, i: (b * nq + i, 0)),
                pl.BlockSpec((lc, kw), lambda b, i: (b, 0)),
                pl.BlockSpec((lc, vw), lambda b, i: (b, 0))]
    args = [q, kc, vc]
    if has_x:
        in_specs += [pl.BlockSpec((lq, kw), lambda b, i: (b, 0)), pl.BlockSpec((lq, vw), lambda b, i: (b, 0))]
        args += [kx, vx]
    return pl.pallas_call(
        functools.partial(_mla_kernel, has_x=has_x),
        grid=(nb, nq),
        in_specs=in_specs,
        out_specs=pl.BlockSpec((tq, vw), lambda b, i: (b * nq + i, 0)),
        out_shape=jax.ShapeDtypeStruct((q.shape[0], vw), BF16),
        compiler_params=_cparams("parallel", "parallel"),
        name="mla_attn_x" if has_x else "mla_attn_c",
    )(*args)


def _gqa_kernel(*refs, has_x, lx, tq):
    if has_x:
        q_ref, kc_ref, vc_ref, kx_ref, vx_ref, sink_ref, o_ref = refs
    else:
        q_ref, kc_ref, vc_ref, sink_ref, o_ref = refs
    rows = GQA_GROUP * tq
    lower = lax.broadcasted_iota(jnp.int32, (1, LANES), 1) < GQA_HD
    slab = 3 * WINDOW

    def chunk(c, carry):
        r0 = pl.multiple_of(c * GQA_ROWS, GQA_ROWS)
        q2 = q_ref[pl.ds(r0, GQA_ROWS), :]
        zero = jnp.zeros_like(q2)
        if has_x:
            t0 = pl.program_id(1) * tq + r0 // GQA_GROUP
            start = pl.multiple_of(jnp.clip((t0 - WINDOW) // WINDOW * WINDOW, 0, lx - slab), LANES)
            kslab = kx_ref[pl.ds(start, slab), :]
            vslab = vx_ref[pl.ds(start, slab), :]
            qpos = t0 + lax.broadcasted_iota(jnp.int32, (GQA_ROWS, slab), 0) // GQA_GROUP
            kpos = start + lax.broadcasted_iota(jnp.int32, (GQA_ROWS, slab), 1)
            valid = jnp.abs(qpos - kpos) <= WINDOW
        outs = []
        for half in range(GQA_KV_HEADS):
            qh = jnp.where(lower, q2, zero) if half == 0 else jnp.where(lower, zero, q2)
            sink = sink_ref[0, pl.ds(r0, GQA_ROWS), half:half + 1]
            sc = _dot_nt(qh, kc_ref[...])
            m = jnp.maximum(jnp.max(sc, axis=-1, keepdims=True), sink)
            if has_x:
                sw = jnp.where(valid, _dot_nt(qh, kslab), -1e30)
                m = jnp.maximum(m, jnp.max(sw, axis=-1, keepdims=True))
            pc = jnp.exp(sc - m)
            den = jnp.sum(pc, axis=-1, keepdims=True) + jnp.exp(sink - m)
            o = _dot(pc.astype(BF16), vc_ref[...])
            if has_x:
                pw = jnp.exp(sw - m)
                den = den + jnp.sum(pw, axis=-1, keepdims=True)
                o = o + _dot(pw.astype(BF16), vslab)
            outs.append(o / den)
        o_ref[pl.ds(r0, GQA_ROWS), :] = jnp.where(lower, outs[0], outs[1]).astype(o_ref.dtype)
        return carry

    lax.fori_loop(0, rows // GQA_ROWS, chunk, 0, unroll=GQA_UNROLL)


def _gqa_attn(gq, gkc, zsc, gkx, zsx, sink_rows, layer, *, nb, tq):
    has_x = gkx is not None
    lq = gq.shape[0] // nb
    lc = gkc.shape[0] // nb
    nq = lq // tq
    kw = GQA_KV_HEADS * GQA_HD
    vcol = ZS_GV // kw
    rows = GQA_GROUP * tq
    q2 = gq.reshape(gq.shape[0] * GQA_GROUP, kw)
    in_specs = [pl.BlockSpec((rows, kw), lambda b, i: (b * nq + i, 0)),
                pl.BlockSpec((lc, kw), lambda b, i: (b, 0)),
                pl.BlockSpec((lc, kw), lambda b, i: (b, vcol))]
    args = [q2, gkc, zsc]
    if has_x:
        assert lq >= 3 * WINDOW and tq % (GQA_ROWS // GQA_GROUP) == 0 and WINDOW % (GQA_ROWS // GQA_GROUP) == 0
        in_specs +=[pl.BlockSpec((lq, kw), lambda b, i: (b, 0)), pl.BlockSpec((lq, kw), lambda b, i: (b, vcol))]
        args += [gkx, zsx]
    assert sink_rows.shape[1] == rows
    in_specs.append(_layer_spec(sink_rows, layer))
    args.append(sink_rows)
    out = pl.pallas_call(
        functools.partial(_gqa_kernel, has_x=has_x, lx=lq, tq=tq),
        grid=(nb, nq),
        in_specs=in_specs,
        out_specs=pl.BlockSpec((rows, kw), lambda b, i: (b * nq + i, 0)),
        out_shape=jax.ShapeDtypeStruct(q2.shape, BF16),
        compiler_params=_cparams("parallel", "parallel"),
        name="gqa_attn_x" if has_x else "gqa_attn_c",
    )(*args)
    return out.reshape(gq.shape)


def _mixers_kernel(z_ref, cw_ref, cb_ref, lg_ref, lb_ref, pw_ref, ps_ref, oc_ref, op_ref, pa_ref, pp_ref, *, seq):
    zpad = jnp.zeros((CONV_PAD, CONV_CH), F32)
    a = z_ref[:, 0:CONV_CH].astype(F32)
    g = z_ref[:, CONV_CH:2 * CONV_CH].astype(F32)
    pa_ref[0:CONV_PAD, :] = zpad
    pa_ref[CONV_PAD + seq:2 * CONV_PAD + seq, :] = zpad
    pa_ref[CONV_PAD:CONV_PAD + seq, :] = a * _sigmoid(g)
    pp_ref[0:CONV_PAD, :] = zpad
    pp_ref[CONV_PAD + seq:2 * CONV_PAD + seq, :] = zpad
    pp_ref[CONV_PAD:CONV_PAD + seq, :] = z_ref[:, 2 * CONV_CH:2 * CONV_CH + POOL_CH].astype(F32)

    def shifted(win, j):
        return win if j == 0 else pltpu.roll(win, win.shape[0] - j, 0)

    def step(c, carry):
        r0 = pl.multiple_of(c * MIX_ROWS, MIX_ROWS)
        win = pa_ref[pl.ds(r0, MIX_ROWS + 2 * CONV_PAD), :]
        acc = jnp.zeros((MIX_ROWS, CONV_CH), F32)
        for j in range(8):
            wj = shifted(win, j)
            for q in range(2 * CONV_PAD // 8):
                k = 8 * q + j - (CONV_PAD - CONV_K // 2)
                if 0 <= k < CONV_K:
                    acc = acc + cw_ref[0, k:k + 1, :] * wj[8 * q:8 * q + MIX_ROWS, :]
        u = acc + cb_ref[0]
        mu = jnp.mean(u, axis=-1, keepdims=True)
        d = u - mu
        y = d * lax.rsqrt(jnp.mean(d * d, axis=-1, keepdims=True) + EPS) * lg_ref[0] + lb_ref[0]
        oc_ref[pl.ds(r0, MIX_ROWS), :] = (y * _sigmoid(y)).astype(BF16)
        pwin = pp_ref[pl.ds(r0 + (CONV_PAD - POOL_REACH), MIX_ROWS + 2 * POOL_REACH), :]
        sums = [jnp.zeros((MIX_ROWS, POOL_GC), F32) for _ in POOL_WINDOWS]
        for j in range(8):
            wj = shifted(pwin, j)
            for q in range(2 * POOL_REACH // 8):
                dlt = 8 * q + j - POOL_REACH
                for gi, w in enumerate(POOL_WINDOWS):
                    if -(w // 2) <= dlt < w - w // 2:
                        sums[gi] = sums[gi] + wj[8 * q:8 * q + MIX_ROWS, gi * POOL_GC:(gi + 1) * POOL_GC]
        t = r0 + lax.broadcasted_iota(jnp.int32, (MIX_ROWS, 1), 0)
        for gi, w in enumerate(POOL_WINDOWS):
            ls = slice(gi * POOL_GC, (gi + 1) * POOL_GC)
            cnt = (jnp.minimum(t - w // 2 + w, seq) - jnp.maximum(t - w // 2, 0)).astype(F32)
            p = sums[gi] / cnt - pwin[POOL_REACH:POOL_REACH + MIX_ROWS, ls]
            op_ref[pl.ds(r0, MIX_ROWS), ls] = (_dot(p.astype(BF16), pw_ref[0, gi]) * ps_ref[0][:, ls]).astype(BF16)
        return carry

    lax.fori_loop(0, seq // MIX_ROWS, step, 0)


def _mixers(zs, cw, cb, lg, lb, pw, ps, layer, *, seq):
    t = zs.shape[0]
    assert t % seq == 0 and seq % MIX_ROWS == 0
    lay = lambda a: _layer_spec(a, layer)
    zw = 2 * CONV_CH + POOL_CH
    out = pl.BlockSpec((seq, CONV_CH), lambda s: (s, 0))
    return pl.pallas_call(
        functools.partial(_mixers_kernel, seq=seq),
        grid=(t // seq,),
        in_specs=[pl.BlockSpec((seq, zw), lambda s: (s, 0)), lay(cw), lay(cb), lay(lg), lay(lb), lay(pw), lay(ps)],
        out_specs=[out, out],
        out_shape=[jax.ShapeDtypeStruct((t, CONV_CH), BF16), jax.ShapeDtypeStruct((t, POOL_CH), BF16)],
        scratch_shapes=[pltpu.VMEM((seq + 2 * CONV_PAD, CONV_CH), F32), pltpu.VMEM((seq + 2 * CONV_PAD, POOL_CH), F32)],
        compiler_params=_cparams("parallel"),
        name="local_mixers",
    )(zs, cw, cb, lg, lb, pw, ps)


def _merge_kernel(oc_ref, om_ref, og_ref, op_ref, sg_ref, wb_ref, y_ref):
    acc = None
    for n, br in enumerate((oc_ref, om_ref, og_ref, op_ref)):
        term = sg_ref[:, n * D_MODEL:(n + 1) * D_MODEL].astype(F32) * _dot(br[...], wb_ref[0, n])
        acc = term if acc is None else acc + term
    y_ref[...] = acc.astype(BF16)


def _merge(branches, sg, wb, layer, *, tm):
    t = sg.shape[0]
    assert t % tm == 0
    br = pl.BlockSpec((tm, BRANCH_W), lambda i: (i, 0))
    return pl.pallas_call(
        _merge_kernel,
        grid=(t // tm,),
        in_specs=[br, br, br, br, pl.BlockSpec((tm, N_BRANCH * D_MODEL), lambda i: (i, 0)), _layer_spec(wb, layer)],
        out_specs=pl.BlockSpec((tm, D_MODEL), lambda i: (i, 0)),
        out_shape=jax.ShapeDtypeStruct((t, D_MODEL), BF16),
        compiler_params=_cparams("parallel"),
        name="gated_merge",
    )(*branches, sg, wb)


def _resid_matmul_kernel(y_ref, w_ref, x_ref, mod_ref, o_ref, *, gt_row):
    o_ref[...] = x_ref[...] + mod_ref[0, 0, gt_row:gt_row + 1, :] * _dot(y_ref[...], w_ref[0])


def _resid_matmul(y, w, x, mod, cond, layer, *, gt_row, tm):
    t, dm = x.shape
    assert t % tm == 0
    return pl.pallas_call(
        functools.partial(_resid_matmul_kernel, gt_row=gt_row),
        grid=(t // tm,),
        in_specs=[pl.BlockSpec((tm, y.shape[1]), lambda i: (i, 0)),
                  _layer_spec(w, layer),
                  pl.BlockSpec((tm, dm), lambda i: (i, 0)),
                  _mod_spec(layer, cond, tm, dm)],
        out_specs=pl.BlockSpec((tm, dm), lambda i: (i, 0)),
        out_shape=jax.ShapeDtypeStruct((t, dm), F32),
        compiler_params=_cparams("parallel"),
        name="out_proj",
    )(y, w, x, mod)


def _ffn_kernel(x_ref, xt_ref, xb_ref, g_ref, mod_ref, wa_ref, wg_ref, cwa_ref, cwg_ref, wd_ref, *rest, seq, final):
    if final:
        fg_ref, o_ref, h_ref = rest
    else:
        o_ref, h_ref = rest
    i, k = pl.program_id(0), pl.program_id(1)
    tm = x_ref.shape[0]
    half = HALO // 2
    multi = tm > seq

    def normmod(x):
        y = _rms(x, g_ref[0])
        return (y * (1.0 + mod_ref[0, 0, _SC2:_SC2 + 1, :]) + mod_ref[0, 0, _SH2:_SH2 + 1, :]).astype(BF16)

    @pl.when(k == 0)
    def _():
        h_ref[HALO:HALO + tm, :] = normmod(x_ref[...])
        nb = normmod(jnp.concatenate([xb_ref[...], xt_ref[...]], axis=0))
        if not multi:
            from_next = lax.broadcasted_iota(jnp.int32, (HALO, 1), 0) < half
            boundary = jnp.where(from_next, ((i + 1) * tm) % seq, (i * tm) % seq)
            nb = jnp.where(boundary == 0, jnp.zeros_like(nb), nb)
        h_ref[0:HALO, :] = nb
        o_ref[...] = x_ref[...]

    if multi:
        pos = lax.broadcasted_iota(jnp.int32, (tm, 1), 0) % seq
        first, last = pos == 0, pos == seq - 1

    def conv(u, cw_ref):
        cur = u[HALO:HALO + tm]
        prev = pltpu.roll(u, 1, 0)[HALO:HALO + tm]
        nxt = pltpu.roll(u, tm + HALO - 1, 0)[HALO:HALO + tm]
        if multi:
            prev = jnp.where(first, 0.0, prev)
            nxt = jnp.where(last, 0.0, nxt)
        return cw_ref[0, 0:1, :] * prev + cw_ref[0, 1:2, :] * cur + cw_ref[0, 2:3, :] * nxt

    h = h_ref[...]
    a = conv(_dot(h, wa_ref[0]), cwa_ref)
    g = conv(_dot(h, wg_ref[0]), cwg_ref)
    act = (g * _sigmoid(g) * a).astype(BF16)
    o_ref[...] += mod_ref[0, 0, _GT2:_GT2 + 1, :] * _dot(act, wd_ref[0])

    if final:
        @pl.when(k == pl.num_programs(1) - 1)
        def _():
            o_ref[...] = _rms(o_ref[...], fg_ref[...])


def _ffn(x, gain, mod, cond, w_up, cw, w_down, layer, final_gain, *, seq, tm, tf):
    t, dm = x.shape
    fp = w_down.shape[1]
    half = HALO // 2
    assert t % tm == 0 and fp % tf == 0 and tm % HALO == 0 and t % seq == 0
    assert seq % tm == 0 or tm % seq == 0
    nk = fp // tf
    hb = tm // half
    last_hb = t // half - 1
    final = final_gain is not None
    in_specs = [pl.BlockSpec((tm, dm), lambda i, k: (i, 0)),
                pl.BlockSpec((half, dm), lambda i, k: (jnp.maximum(i * hb - 1, 0), 0)),
                pl.BlockSpec((half, dm), lambda i, k: (jnp.minimum((i + 1) * hb, last_hb), 0)),
                _layer_spec(gain, layer),
                _mod_spec(layer, cond, tm, dm),
                pl.BlockSpec((1, dm, tf), lambda i, k: (layer, 0, k)),
                pl.BlockSpec((1, dm, tf), lambda i, k: (layer, 0, nk + k)),
                pl.BlockSpec((1, 8, tf), lambda i, k: (layer, 0, k)),
                pl.BlockSpec((1, 8, tf), lambda i, k: (layer, 0, nk + k)),
                pl.BlockSpec((1, tf, dm), lambda i, k: (layer, k, 0))]
    args = [x, x, x, gain, mod, w_up, w_up, cw, cw, w_down]
    if final:
        in_specs.append(pl.BlockSpec((1, dm), lambda i, k: (0, 0)))
        args.append(final_gain)
    return pl.pallas_call(
        functools.partial(_ffn_kernel, seq=seq, final=final),
        grid=(t // tm, nk),
        in_specs=in_specs,
        out_specs=pl.BlockSpec((tm, dm), lambda i, k: (i, 0)),
        out_shape=jax.ShapeDtypeStruct((t, dm), F32),
        scratch_shapes=[pltpu.VMEM((tm + HALO, dm), BF16)],
        compiler_params=_cparams("parallel", "arbitrary"),
        name="conv_ffn",
    )(*args)


def _repack_kernel(x_ref, *o_refs, plans):
    rows = x_ref.shape[1]
    width = x_ref.shape[2]
    for o_ref, plan in zip(o_refs, plans):
        parts = []
        for seg in plan:
            if seg[0] == "zero":
                parts.append(jnp.zeros((rows, seg[1]), BF16))
            else:
                a, b = seg
                lo, hi = a // LANES * LANES, min(-(-b // LANES) * LANES, width)
                parts.append(x_ref[0, :, lo:hi][:, a - lo:b - lo].astype(BF16))
        o_ref[0] = parts[0] if len(parts) == 1 else jnp.concatenate(parts, axis=1)


def _repack(w, plans, *, tr, name):
    depth, rows, width = w.shape
    assert rows % tr == 0
    widths = [sum(s[1] if s[0] == "zero" else s[1] - s[0] for s in plan) for plan in plans]
    return pl.pallas_call(
        functools.partial(_repack_kernel, plans=plans),
        grid=(depth, rows // tr),
        in_specs=[pl.BlockSpec((1, tr, width), lambda l, r: (l, r, 0))],
        out_specs=[pl.BlockSpec((1, tr, n), lambda l, r: (l, r, 0)) for n in widths],
        out_shape=[jax.ShapeDtypeStruct((depth, rows, n), BF16) for n in widths],
        compiler_params=_cparams("parallel", "parallel"),
        name=name,
    )(w)


def _rope_tables(seq):
    t = jnp.arange(seq)
    row = (t // GRID_W).astype(F32)
    col = (t % GRID_W).astype(F32)
    half = ROPE_DIM // 2
    inv_freq = ROPE_THETA ** (-jnp.arange(0, half, 2, dtype=F32) / half)
    d = jnp.arange(ROPE_DIM)
    ang = jnp.where((d < half)[None, :], row[:, None], col[:, None]) * inv_freq[d % (half // 2)][None, :]
    upper = ((d % half) < half // 2)[None, :]
    cos = jnp.cos(ang)
    sin = jnp.sin(ang)
    sa = jnp.where(upper, -sin, 0.0)
    sb = jnp.where(upper, 0.0, sin)
    rep = lambda a: jnp.tile(a, (1, LANES // ROPE_DIM)).astype(F32)
    return rep(cos), rep(sa), rep(sb)


def _identity_tables(rows):
    return jnp.ones((rows, LANES), F32), jnp.zeros((rows, LANES), F32), jnp.zeros((rows, LANES), F32)


def _group_major(a, axis):
    order = [kv * GQA_GROUP + g for g in range(GQA_GROUP) for kv in range(GQA_KV_HEADS)]
    take = lambda h: lax.slice_in_dim(a, h * GQA_HD, (h + 1) * GQA_HD, axis=axis)
    return jnp.concatenate([take(h) for h in order], axis=axis)


def _stacked_weights(w_in, mla_w_uq, mla_w_ukv, w_branch, w_out, ffn_w_up, ffn_conv_w, ffn_w_down, pool_w):
    zeros = lambda arr, n: jnp.zeros(arr.shape[:-1] + (n,), BF16)
    gq0 = 1344
    gq_heads = [(gq0 + (kv * GQA_GROUP + g) * GQA_HD, gq0 + (kv * GQA_GROUP + g + 1) * GQA_HD)
                for g in range(GQA_GROUP) for kv in range(GQA_KV_HEADS)]
    small_plan = [(1856, 2880), (2880, 3392), (0, 512), (832, 1344), *gq_heads, (576, 704), (704, 832), (512, 576),
                  ("zero", ZS_COLS - ZS_KR - ROPE_DIM)]
    gate0 = 3392
    w_small, w_gate = _repack(w_in, [small_plan, [(gate0, gate0 + N_BRANCH * D_MODEL)]], tr=128, name="repack_w_in")
    uq, ukv = mla_w_uq.astype(BF16), mla_w_ukv.astype(BF16)
    wq, wk, wv = [], [], []
    for h in range(MLA_HEADS):
        wq += [uq[:, :, h * MLA_QK:(h + 1) * MLA_QK], zeros(uq, MLA_SLOT - MLA_QK)]
        lo = h * (MLA_NOPE + MLA_V)
        wk += [ukv[:, :, lo:lo + MLA_NOPE], zeros(ukv, MLA_SLOT - MLA_NOPE)]
        wv.append(ukv[:, :, lo + MLA_NOPE:lo + MLA_NOPE + MLA_V])
    wq, wk, wv = (jnp.concatenate(parts, axis=2) for parts in (wq, wk, wv))
    wb = w_branch.astype(BF16)
    wb = jnp.concatenate([wb[:, :2], _group_major(wb[:, 2], 1)[:, None], wb[:, 3:]], axis=1)
    padc = FF_PAD - D_FF
    (w_up,) = _repack(ffn_w_up, [[(0, D_FF), ("zero", padc), (D_FF, 2 * D_FF), ("zero", padc)]], tr=128,
                      name="repack_w_up")
    cwz = jnp.zeros(ffn_conv_w.shape[:-1] + (padc,), ffn_conv_w.dtype)
    cw = jnp.concatenate([ffn_conv_w[:, :, :D_FF], cwz, ffn_conv_w[:, :, D_FF:], cwz], axis=2)
    cw = jnp.pad(cw, ((0, 0), (0, 8 - cw.shape[1]), (0, 0)))
    w_down = jnp.concatenate([ffn_w_down.astype(BF16), jnp.zeros((ffn_w_down.shape[0], padc, D_MODEL), BF16)], axis=1)
    return dict(w_small=w_small, w_gate=w_gate, wq=wq, wk=wk, wv=wv, w_branch=wb, w_out=w_out.astype(BF16),
                w_up=w_up, ffn_cw=cw, w_down=w_down, pool_w=pool_w.astype(BF16))


def _rope_placement():
    d = jnp.arange(LANES)[:, None]
    col = jnp.arange(MLA_HEADS * MLA_SLOT)[None, :]
    return ((col % MLA_SLOT == MLA_NOPE + d) & (d < ROPE_DIM)).astype(BF16)


def _sink_rows(gqa_sink, tq):
    depth = gqa_sink.shape[0]
    s = gqa_sink.reshape(depth, GQA_KV_HEADS, GQA_GROUP).transpose(0, 2, 1)
    s = jnp.pad(s, ((0, 0), (0, 0), (0, LANES - GQA_KV_HEADS)))
    return jnp.tile(s, (1, tq, 1))


def _stream_front(x, mod, cond, sw, p, tabs, tab_period, layer, *, tm, gates):
    if gates:
        zs, sg = _in_proj(x, p["norm1_g"], mod, cond, sw["w_small"], sw["w_gate"], layer, tm=tm,
                          tn_small=ZS_TN, tn_gate=1024)
    else:
        zs, sg = _nm_matmul(x, p["norm1_g"], mod, cond, sw["w_small"], layer, tm=tm, tn=ZS_TN,
                            name="in_proj_small"), None
    q, k, v, gq, gk = _prep(zs, tabs, tab_period, p["mla_kv_norm"], p["mla_q_norm"], sw["wk"], sw["wv"], sw["wq"],
                            p["rope_place"], layer, tm=min(tm, 512))
    return zs, sg, q, k, v, gq, gk


def _stream_back(x, mod, cond, sw, p, zs, sg, o_mla, o_gqa, layer, final_gain, *, tm, seq):
    o_conv, o_pool = _mixers(zs, p["conv_w"], p["conv_b"], p["conv_ln_g"], p["conv_ln_b"], sw["pool_w"],
                             p["pool_scale"], layer, seq=seq)
    y = _merge((o_conv, o_mla, o_gqa, o_pool), sg, sw["w_branch"], layer, tm=min(tm, 256))
    x = _resid_matmul(y, sw["w_out"], x, mod, cond, layer, gt_row=_GT1, tm=min(tm, 512))
    return _ffn(x, p["norm2_g"], mod, cond, sw["w_up"], sw["ffn_cw"], sw["w_down"], layer, final_gain,
                seq=seq, tm=min(tm, 512), tf=512)


def kernel(x, c, ctx, c_ctx, norm1_g, norm2_g, w_ada, b_ada, w_in, mla_q_norm, mla_w_uq, mla_kv_norm, mla_w_ukv,
           gqa_sink, conv_w, conv_b, conv_ln_g, conv_ln_b, pool_w, pool_scale, w_branch, w_out, ffn_w_up,
           ffn_conv_w, ffn_w_down, final_norm_g):
    nb, seq, dm = x.shape
    lc = ctx.shape[1]
    tx, tc = nb * seq, nb * lc
    tm_x = min(1024, seq)
    tm_c = min(1024, tc)
    tq_x = min(256, seq - 2 * WINDOW)
    tq_c = min(256, lc)

    cs = jnp.zeros((16, dm), F32).at[:nb].set(c).at[nb].set(c_ctx)
    mod = _ada_mod(cs, w_ada, b_ada).reshape(DEPTH, 16, N_MOD, dm)
    mod = jnp.pad(mod, ((0, 0), (0, 0), (0, 8 - N_MOD), (0, 0)))

    sw = _stacked_weights(w_in, mla_w_uq, mla_w_ukv, w_branch, w_out, ffn_w_up, ffn_conv_w, ffn_w_down, pool_w)
    vec = lambda a: a[:, None, :]
    p = dict(norm1_g=vec(norm1_g), norm2_g=vec(norm2_g), mla_kv_norm=vec(mla_kv_norm), mla_q_norm=vec(mla_q_norm),
             rope_place=_rope_placement(), conv_w=conv_w, conv_b=vec(conv_b), conv_ln_g=vec(conv_ln_g),
             conv_ln_b=vec(conv_ln_b), pool_scale=vec(pool_scale))
    tabs_x = _rope_tables(seq)
    tabs_c = _identity_tables(min(tm_c, 512))
    sink_x = _sink_rows(gqa_sink, tq_x)
    sink_c = _sink_rows(gqa_sink, tq_c)

    xs = x.reshape(tx, dm)
    cx = ctx.reshape(tc, dm)
    cond_x = (seq, 0)
    cond_c = (tc, nb)
    for l in range(DEPTH):
        last = l == DEPTH - 1
        zs_c, sg_c, q_c, k_c, v_c, gq_c, gk_c = _stream_front(cx, mod, cond_c, sw, p, tabs_c, 1, l, tm=tm_c,
                                                              gates=not last)
        zs_x, sg_x, q_x, k_x, v_x, gq_x, gk_x = _stream_front(xs, mod, cond_x, sw, p, tabs_x,
                                                              seq // min(tm_x, 512), l, tm=tm_x, gates=True)
        o_mla = _mla_attn(q_x, k_c, v_c, k_x, v_x, nb=nb, tq=min(512, seq))
        o_gqa = _gqa_attn(gq_x, gk_c, zs_c, gk_x, zs_x, sink_x, l, nb=nb, tq=tq_x)
        xs = _stream_back(xs, mod, cond_x, sw, p, zs_x, sg_x, o_mla, o_gqa, l,
                          final_norm_g[None] if last else None, tm=tm_x, seq=seq)
        if not last:
            o_mla_c = _mla_attn(q_c, k_c, v_c, None, None, nb=nb, tq=min(256, lc))
            o_gqa_c = _gqa_attn(gq_c, gk_c, zs_c, None, None, sink_c, l, nb=nb, tq=tq_c)
            cx = _stream_back(cx, mod, cond_c, sw, p, zs_c, sg_c, o_mla_c, o_gqa_c, l, None, tm=tm_c, seq=lc)

    return xs.reshape(nb, seq, dm)
```

```python
import functools

import jax
import jax.numpy as jnp
from jax import lax
from jax.experimental import pallas as pl
from jax.experimental.pallas import tpu as pltpu

F32 = jnp.float32
BF16 = jnp.bfloat16

D_MODEL = 2048
DEPTH = 4
GRID_W = 64
EPS = 1e-6
ROPE_THETA = 10000.0
ROPE_DIM = 64
MLA_HEADS = 4
MLA_NOPE = 128
MLA_V = 128
MLA_RANK = 512
MLA_QK = MLA_NOPE + ROPE_DIM
GQA_HEADS = 8
GQA_KV_HEADS = 2
GQA_GROUP = GQA_HEADS // GQA_KV_HEADS
GQA_HD = 64
WINDOW = 128
CONV_CH = 512
CONV_K = 31
POOL_WINDOWS = (2, 4, 8, 16)
POOL_GC = 128
POOL_CH = 512
N_BRANCH = 4
BRANCH_W = 512
D_FF = 5504
N_MOD = 6

LANES = 128
VMEM_LIMIT_BYTES = 56 * 1024 * 1024

ZS_CONV = 0
ZS_POOL = 1024
ZS_CKV = 1536
ZS_CQ = 2048
ZS_GQ = 2560
ZS_GK = 3072
ZS_GV = 3200
ZS_KR = 3328
ZS_COLS = 3584
ZS_TN = 512
MLA_SLOT = 256
FF_PAD = 5632
HALO = 16
CONV_PAD = 16
POOL_REACH = 8
MIX_ROWS = 64
GQA_TOK = 64
GQA_ROWS = GQA_TOK * GQA_GROUP
GQA_UNROLL = 4

_SH1, _SC1, _GT1, _SH2, _SC2, _GT2 = range(N_MOD)


def _cparams(*sem):
    return pltpu.CompilerParams(dimension_semantics=sem, vmem_limit_bytes=VMEM_LIMIT_BYTES)


def _dot(a, b):
    return jnp.dot(a, b, preferred_element_type=F32)


def _dot_nt(a, b):
    return lax.dot_general(a, b, (((1,), (1,)), ((), ())), preferred_element_type=F32)


def _sigmoid(v):
    return 1.0 / (1.0 + jnp.exp(-v))


def _rms(v, g):
    return v * lax.rsqrt(jnp.mean(v * v, axis=-1, keepdims=True) + EPS) * g


def _layer_spec(arr, layer):
    zeros = (0,) * (arr.ndim - 1)
    return pl.BlockSpec((1,) + arr.shape[1:], lambda *_: (layer,) + zeros)


def _mod_spec(layer, cond, tm, dm):
    per, base = cond
    return pl.BlockSpec((1, 1, 8, dm), lambda i, *_: (layer, base + (i * tm) // per, 0, 0))


def _ada_kernel(c_ref, w_ref, b_ref, o_ref):
    c = c_ref[...]
    s = (c * _sigmoid(c)).astype(BF16)
    o_ref[0] = _dot(s, w_ref[0].astype(BF16)) + b_ref[0]


def _ada_mod(cs, w_ada, b_ada):
    depth, dm, n = w_ada.shape
    tn = 1024
    return pl.pallas_call(
        _ada_kernel,
        grid=(depth, n // tn),
        in_specs=[pl.BlockSpec((cs.shape[0], dm), lambda l, j: (0, 0)),
                  pl.BlockSpec((1, dm, tn), lambda l, j: (l, 0, j)),
                  pl.BlockSpec((1, 1, tn), lambda l, j: (l, 0, j))],
        out_specs=pl.BlockSpec((1, cs.shape[0], tn), lambda l, j: (l, 0, j)),
        out_shape=jax.ShapeDtypeStruct((depth, cs.shape[0], n), F32),
        compiler_params=_cparams("parallel", "parallel"),
        name="ada_mod",
    )(cs, w_ada, b_ada.reshape(depth, 1, n))


def _norm1(x_ref, g_ref, mod_ref):
    y = _rms(x_ref[...], g_ref[0])
    return (y * (1.0 + mod_ref[0, 0, _SC1:_SC1 + 1, :]) + mod_ref[0, 0, _SH1:_SH1 + 1, :]).astype(BF16)


def _nm_matmul_kernel(x_ref, g_ref, mod_ref, w_ref, o_ref, h_ref):
    @pl.when(pl.program_id(1) == 0)
    def _():
        h_ref[...] = _norm1(x_ref, g_ref, mod_ref)

    o_ref[...] = _dot(h_ref[...], w_ref[0]).astype(o_ref.dtype)


def _nm_matmul(x, gain, mod, cond, w, layer, *, tm, tn, name):
    t, dm = x.shape
    n = w.shape[2]
    assert t % tm == 0 and n % tn == 0
    return pl.pallas_call(
        _nm_matmul_kernel,
        grid=(t // tm, n // tn),
        in_specs=[pl.BlockSpec((tm, dm), lambda i, j: (i, 0)),
                  _layer_spec(gain, layer),
                  _mod_spec(layer, cond, tm, dm),
                  pl.BlockSpec((1, dm, tn), lambda i, j: (layer, 0, j))],
        out_specs=pl.BlockSpec((tm, tn), lambda i, j: (i, j)),
        out_shape=jax.ShapeDtypeStruct((t, n), BF16),
        scratch_shapes=[pltpu.VMEM((tm, dm), BF16)],
        compiler_params=_cparams("parallel", "arbitrary"),
        name=name,
    )(x, gain, mod, w)


def _in_proj_kernel(x_ref, g_ref, mod_ref, ws_ref, wg_ref, zs_ref, sg_ref, h_ref, *, ns):
    j = pl.program_id(1)

    @pl.when(j == 0)
    def _():
        h_ref[...] = _norm1(x_ref, g_ref, mod_ref)

    @pl.when(j < ns)
    def _():
        zs_ref[...] = _dot(h_ref[...], ws_ref[0]).astype(BF16)

    @pl.when(j >= ns)
    def _():
        sg_ref[...] = _sigmoid(_dot(h_ref[...], wg_ref[0])).astype(BF16)


def _in_proj(x, gain, mod, cond, w_small, w_gate, layer, *, tm, tn_small, tn_gate):
    t, dm = x.shape
    n_small, n_gate = w_small.shape[2], w_gate.shape[2]
    assert t % tm == 0 and n_small % tn_small == 0 and n_gate % tn_gate == 0
    ns, ng = n_small // tn_small, n_gate // tn_gate
    small_blk = lambda j: jnp.minimum(j, ns - 1)
    gate_blk = lambda j: jnp.maximum(j - ns, 0)
    return pl.pallas_call(
        functools.partial(_in_proj_kernel, ns=ns),
        grid=(t // tm, ns + ng),
        in_specs=[pl.BlockSpec((tm, dm), lambda i, j: (i, 0)),
                  _layer_spec(gain, layer),
                  _mod_spec(layer, cond, tm, dm),
                  pl.BlockSpec((1, dm, tn_small), lambda i, j: (layer, 0, small_blk(j))),
                  pl.BlockSpec((1, dm, tn_gate), lambda i, j: (layer, 0, gate_blk(j)))],
        out_specs=[pl.BlockSpec((tm, tn_small), lambda i, j: (i, small_blk(j))),
                   pl.BlockSpec((tm, tn_gate), lambda i, j: (i, gate_blk(j)))],
        out_shape=[jax.ShapeDtypeStruct((t, n_small), BF16), jax.ShapeDtypeStruct((t, n_gate), BF16)],
        scratch_shapes=[pltpu.VMEM((tm, dm), BF16)],
        compiler_params=_cparams("parallel", "arbitrary"),
        name="in_proj",
    )(x, gain, mod, w_small, w_gate)


def _rope(v, cos, sa, sb):
    return v * cos + pltpu.roll(v, LANES - 16, 1) * sa + pltpu.roll(v, 16, 1) * sb


def _prep_kernel(ckv_ref, cq_ref, gq_ref, tail_ref, cos_ref, sa_ref, sb_ref, kvn_ref, qn_ref,
                 wk_ref, wv_ref, wq_ref, e_ref, q_out, k_out, v_out, gq_out, gk_out):
    cos, sa, sb = cos_ref[...], sa_ref[...], sb_ref[...]
    nkv = _rms(ckv_ref[...].astype(F32), kvn_ref[0]).astype(BF16)
    kr = _rope(tail_ref[:, 2 * LANES:3 * LANES].astype(F32), cos, sa, sb).astype(BF16)
    k_out[...] = (_dot(nkv, wk_ref[0]) + _dot(kr, e_ref[...])).astype(BF16)
    v_out[...] = _dot(nkv, wv_ref[0]).astype(BF16)

    nq = _rms(cq_ref[...].astype(F32), qn_ref[0]).astype(BF16)
    q = _dot(nq, wq_ref[0])
    scale = MLA_QK ** -0.5
    for h in range(MLA_HEADS):
        lo = h * MLA_SLOT
        q_out[:, lo:lo + LANES] = (q[:, lo:lo + LANES] * scale).astype(BF16)
        q_out[:, lo + LANES:lo + 2 * LANES] = (_rope(q[:, lo + LANES:lo + 2 * LANES], cos, sa, sb) * scale).astype(BF16)

    for c in range(GQA_HEADS * GQA_HD // LANES):
        sl = slice(c * LANES, (c + 1) * LANES)
        gq_out[:, sl] = (_rope(gq_ref[:, sl].astype(F32), cos, sa, sb) * GQA_HD ** -0.5).astype(BF16)
    gk_out[...] = _rope(tail_ref[:, 0:LANES].astype(F32), cos, sa, sb).astype(BF16)


def _prep(zs, tabs, tab_period, kvn, qn, wk, wv, wq, e, layer, *, tm):
    t = zs.shape[0]
    assert t % tm == 0
    cb = lambda off, w: pl.BlockSpec((tm, w), lambda i: (i, off // w))
    tab = pl.BlockSpec((tm, LANES), lambda i: (i % tab_period, 0))
    lay = lambda a: _layer_spec(a, layer)
    row = lambda w: pl.BlockSpec((tm, w), lambda i: (i, 0))
    widths = (MLA_HEADS * MLA_SLOT, MLA_HEADS * MLA_SLOT, MLA_HEADS * MLA_V, GQA_HEADS * GQA_HD, GQA_KV_HEADS * GQA_HD)
    return pl.pallas_call(
        _prep_kernel,
        grid=(t // tm,),
        in_specs=[cb(ZS_CKV, MLA_RANK), cb(ZS_CQ, MLA_RANK), cb(ZS_GQ, GQA_HEADS * GQA_HD), cb(ZS_GK, 3 * LANES),
                  tab, tab, tab, lay(kvn), lay(qn), lay(wk), lay(wv), lay(wq),
                  pl.BlockSpec(e.shape, lambda i: (0, 0))],
        out_specs=[row(w) for w in widths],
        out_shape=[jax.ShapeDtypeStruct((t, w), BF16) for w in widths],
        compiler_params=_cparams("parallel"),
        name="qkv_prep",
    )(zs, zs, zs, zs, *tabs, kvn, qn, wk, wv, wq, e)


def _mla_kernel(*refs, has_x):
    if has_x:
        q_ref, kc_ref, vc_ref, kx_ref, vx_ref, o_ref = refs
    else:
        q_ref, kc_ref, vc_ref, o_ref = refs
    for h in range(MLA_HEADS):
        ks = slice(h * MLA_SLOT, (h + 1) * MLA_SLOT)
        vs = slice(h * MLA_V, (h + 1) * MLA_V)
        qh = q_ref[:, ks]
        sc = _dot_nt(qh, kc_ref[:, ks])
        m = jnp.max(sc, axis=-1, keepdims=True)
        if has_x:
            sx = _dot_nt(qh, kx_ref[:, ks])
            m = jnp.maximum(m, jnp.max(sx, axis=-1, keepdims=True))
        pc = jnp.exp(sc - m)
        den = jnp.sum(pc, axis=-1, keepdims=True)
        o = _dot(pc.astype(BF16), vc_ref[:, vs])
        if has_x:
            px = jnp.exp(sx - m)
            den = den + jnp.sum(px, axis=-1, keepdims=True)
            o = o + _dot(px.astype(BF16), vx_ref[:, vs])
        o_ref[:, vs] = (o / den).astype(o_ref.dtype)


def _mla_attn(q, kc, vc, kx, vx, *, nb, tq):
    has_x = kx is not None
    lq = q.shape[0] // nb
    lc = kc.shape[0] // nb
    nq = lq // tq
    kw, vw = MLA_HEADS * MLA_SLOT, MLA_HEADS * MLA_V
    in_specs = [pl.BlockSpec((tq, kw), lambda b, i: (b * nq + i, 0)),
                pl.BlockSpec((lc, kw), lambda b, i: (b, 0)),
                pl.BlockSpec((lc, vw), lambda b, i: (b, 0))]
    args = [q, kc, vc]
    if has_x:
        in_specs += [pl.BlockSpec((lq, kw), lambda b, i: (b, 0)), pl.BlockSpec((lq, vw), lambda b, i: (b, 0))]
        args += [kx, vx]
    return pl.pallas_call(
        functools.partial(_mla_kernel, has_x=has_x),
        grid=(nb, nq),
        in_specs=in_specs,
        out_specs=pl.BlockSpec((tq, vw), lambda b, i: (b * nq + i, 0)),
        out_shape=jax.ShapeDtypeStruct((q.shape[0], vw), BF16),
        compiler_params=_cparams("parallel", "parallel"),
        name="mla_attn_x" if has_x else "mla_attn_c",
    )(*args)


def _gqa_kernel(*refs, has_x, lx, tq):
    if has_x:
        q_ref, kc_ref, vc_ref, kx_ref, vx_ref, sink_ref, o_ref = refs
    else:
        q_ref, kc_ref, vc_ref, sink_ref, o_ref = refs
    lower = lax.broadcasted_iota(jnp.int32, (1, LANES), 1) < GQA_HD
    slab = 3 * WINDOW

    def chunk(c, carry):
        ts = pl.multiple_of(c * GQA_TOK, GQA_TOK)
        q2 = jnp.concatenate([q_ref[pl.ds(ts, GQA_TOK), g * LANES:(g + 1) * LANES] for g in range(GQA_GROUP)], axis=0)
        zero = jnp.zeros_like(q2)
        if has_x:
            t0 = pl.program_id(1) * tq + ts
            start = pl.multiple_of(jnp.clip((t0 - WINDOW) // WINDOW * WINDOW, 0, lx - slab), LANES)
            kslab = kx_ref[pl.ds(start, slab), :]
            vslab = vx_ref[pl.ds(start, slab), :]
            qpos = t0 + lax.broadcasted_iota(jnp.int32, (GQA_ROWS, slab), 0) % GQA_TOK
            kpos = start + lax.broadcasted_iota(jnp.int32, (GQA_ROWS, slab), 1)
            valid = jnp.abs(qpos - kpos) <= WINDOW
        outs = []
        for half in range(GQA_KV_HEADS):
            qh = jnp.where(lower, q2, zero) if half == 0 else jnp.where(lower, zero, q2)
            sink = sink_ref[0, :, half:half + 1]
            sc = _dot_nt(qh, kc_ref[...])
            m = jnp.maximum(jnp.max(sc, axis=-1, keepdims=True), sink)
            if has_x:
                sw = jnp.where(valid, _dot_nt(qh, kslab), -1e30)
                m = jnp.maximum(m, jnp.max(sw, axis=-1, keepdims=True))
            pc = jnp.exp(sc - m)
            den = jnp.sum(pc, axis=-1, keepdims=True) + jnp.exp(sink - m)
            o = _dot(pc.astype(BF16), vc_ref[...])
            if has_x:
                pw = jnp.exp(sw - m)
                den = den + jnp.sum(pw, axis=-1, keepdims=True)
                o = o + _dot(pw.astype(BF16), vslab)
            outs.append(o / den)
        o2 = jnp.where(lower, outs[0], outs[1]).astype(o_ref.dtype)
        for g in range(GQA_GROUP):
            o_ref[pl.ds(ts, GQA_TOK), g * LANES:(g + 1) * LANES] = o2[g * GQA_TOK:(g + 1) * GQA_TOK]
        return carry

    lax.fori_loop(0, tq // GQA_TOK, chunk, 0, unroll=min(GQA_UNROLL, tq // GQA_TOK))


def _gqa_attn(gq, gkc, zsc, gkx, zsx, sink_rows, layer, *, nb, tq):
    has_x = gkx is not None
    lq = gq.shape[0] // nb
    lc = gkc.shape[0] // nb
    nq = lq // tq
    qw, kw = GQA_HEADS * GQA_HD, GQA_KV_HEADS * GQA_HD
    vcol = ZS_GV // kw
    assert tq % GQA_TOK == 0
    in_specs = [pl.BlockSpec((tq, qw), lambda b, i: (b * nq + i, 0)),
                pl.BlockSpec((lc, kw), lambda b, i: (b, 0)),
                pl.BlockSpec((lc, kw), lambda b, i: (b, vcol))]
    args = [gq, gkc, zsc]
    if has_x:
        assert lq >= 3 * WINDOW and WINDOW % GQA_TOK == 0
        in_specs += [pl.BlockSpec((lq, kw), lambda b, i: (b, 0)), pl.BlockSpec((lq, kw), lambda b, i: (b, vcol))]
        args += [gkx, zsx]
    in_specs.append(_layer_spec(sink_rows, layer))
    args.append(sink_rows)
    return pl.pallas_call(
        functools.partial(_gqa_kernel, has_x=has_x, lx=lq, tq=tq),
        grid=(nb, nq),
        in_specs=in_specs,
        out_specs=pl.BlockSpec((tq, qw), lambda b, i: (b * nq + i, 0)),
        out_shape=jax.ShapeDtypeStruct(gq.shape, BF16),
        compiler_params=_cparams("parallel", "parallel"),
        name="gqa_attn_x" if has_x else "gqa_attn_c",
    )(*args)


def _mixers_kernel(z_ref, cw_ref, cb_ref, lg_ref, lb_ref, pw_ref, ps_ref, oc_ref, op_ref, pa_ref, pp_ref, *, seq):
    zpad = jnp.zeros((CONV_PAD, CONV_CH), F32)
    a = z_ref[:, 0:CONV_CH].astype(F32)
    g = z_ref[:, CONV_CH:2 * CONV_CH].astype(F32)
    pa_ref[0:CONV_PAD, :] = zpad
    pa_ref[CONV_PAD + seq:2 * CONV_PAD + seq, :] = zpad
    pa_ref[CONV_PAD:CONV_PAD + seq, :] = a * _sigmoid(g)
    pp_ref[0:CONV_PAD, :] = zpad
    pp_ref[CONV_PAD + seq:2 * CONV_PAD + seq, :] = zpad
    pp_ref[CONV_PAD:CONV_PAD + seq, :] = z_ref[:, 2 * CONV_CH:2 * CONV_CH + POOL_CH].astype(F32)

    def shifted(win, j):
        return win if j == 0 else pltpu.roll(win, win.shape[0] - j, 0)

    def step(c, carry):
        r0 = pl.multiple_of(c * MIX_ROWS, MIX_ROWS)
        win = pa_ref[pl.ds(r0, MIX_ROWS + 2 * CONV_PAD), :]
        acc = jnp.zeros((MIX_ROWS, CONV_CH), F32)
        for j in range(8):
            wj = shifted(win, j)
            for q in range(2 * CONV_PAD // 8):
                k = 8 * q + j - (CONV_PAD - CONV_K // 2)
                if 0 <= k < CONV_K:
                    acc = acc + cw_ref[0, k:k + 1, :] * wj[8 * q:8 * q + MIX_ROWS, :]
        u = acc + cb_ref[0]
        mu = jnp.mean(u, axis=-1, keepdims=True)
        d = u - mu
        y = d * lax.rsqrt(jnp.mean(d * d, axis=-1, keepdims=True) + EPS) * lg_ref[0] + lb_ref[0]
        oc_ref[pl.ds(r0, MIX_ROWS), :] = (y * _sigmoid(y)).astype(BF16)
        pwin = pp_ref[pl.ds(r0 + (CONV_PAD - POOL_REACH), MIX_ROWS + 2 * POOL_REACH), :]
        sums = [jnp.zeros((MIX_ROWS, POOL_GC), F32) for _ in POOL_WINDOWS]
        for j in range(8):
            wj = shifted(pwin, j)
            for q in range(2 * POOL_REACH // 8):
                dlt = 8 * q + j - POOL_REACH
                for gi, w in enumerate(POOL_WINDOWS):
                    if -(w // 2) <= dlt < w - w // 2:
                        sums[gi] = sums[gi] + wj[8 * q:8 * q + MIX_ROWS, gi * POOL_GC:(gi + 1) * POOL_GC]
        t = r0 + lax.broadcasted_iota(jnp.int32, (MIX_ROWS, 1), 0)
        for gi, w in enumerate(POOL_WINDOWS):
            ls = slice(gi * POOL_GC, (gi + 1) * POOL_GC)
            cnt = (jnp.minimum(t - w // 2 + w, seq) - jnp.maximum(t - w // 2, 0)).astype(F32)
            p = sums[gi] / cnt - pwin[POOL_REACH:POOL_REACH + MIX_ROWS, ls]
            op_ref[pl.ds(r0, MIX_ROWS), ls] = (_dot(p.astype(BF16), pw_ref[0, gi]) * ps_ref[0][:, ls]).astype(BF16)
        return carry

    lax.fori_loop(0, seq // MIX_ROWS, step, 0)


def _mixers(zs, cw, cb, lg, lb, pw, ps, layer, *, seq):
    t = zs.shape[0]
    assert t % seq == 0 and seq % MIX_ROWS == 0
    lay = lambda a: _layer_spec(a, layer)
    zw = 2 * CONV_CH + POOL_CH
    out = pl.BlockSpec((seq, CONV_CH), lambda s: (s, 0))
    return pl.pallas_call(
        functools.partial(_mixers_kernel, seq=seq),
        grid=(t // seq,),
        in_specs=[pl.BlockSpec((seq, zw), lambda s: (s, 0)), lay(cw), lay(cb), lay(lg), lay(lb), lay(pw), lay(ps)],
        out_specs=[out, out],
        out_shape=[jax.ShapeDtypeStruct((t, CONV_CH), BF16), jax.ShapeDtypeStruct((t, POOL_CH), BF16)],
        scratch_shapes=[pltpu.VMEM((seq + 2 * CONV_PAD, CONV_CH), F32), pltpu.VMEM((seq + 2 * CONV_PAD, POOL_CH), F32)],
        compiler_params=_cparams("parallel"),
        name="local_mixers",
    )(zs, cw, cb, lg, lb, pw, ps)


def _merge_kernel(oc_ref, om_ref, og_ref, op_ref, sg_ref, wb_ref, y_ref):
    acc = None
    for n, br in enumerate((oc_ref, om_ref, og_ref, op_ref)):
        term = sg_ref[:, n * D_MODEL:(n + 1) * D_MODEL].astype(F32) * _dot(br[...], wb_ref[0, n])
        acc = term if acc is None else acc + term
    y_ref[...] = acc.astype(BF16)


def _merge(branches, sg, wb, layer, *, tm):
    t = sg.shape[0]
    assert t % tm == 0
    br = pl.BlockSpec((tm, BRANCH_W), lambda i: (i, 0))
    return pl.pallas_call(
        _merge_kernel,
        grid=(t // tm,),
        in_specs=[br, br, br, br, pl.BlockSpec((tm, N_BRANCH * D_MODEL), lambda i: (i, 0)), _layer_spec(wb, layer)],
        out_specs=pl.BlockSpec((tm, D_MODEL), lambda i: (i, 0)),
        out_shape=jax.ShapeDtypeStruct((t, D_MODEL), BF16),
        compiler_params=_cparams("parallel"),
        name="gated_merge",
    )(*branches, sg, wb)


def _resid_matmul_kernel(y_ref, w_ref, x_ref, mod_ref, o_ref, *, gt_row):
    o_ref[...] = x_ref[...] + mod_ref[0, 0, gt_row:gt_row + 1, :] * _dot(y_ref[...], w_ref[0])


def _resid_matmul(y, w, x, mod, cond, layer, *, gt_row, tm):
    t, dm = x.shape
    assert t % tm == 0
    return pl.pallas_call(
        functools.partial(_resid_matmul_kernel, gt_row=gt_row),
        grid=(t // tm,),
        in_specs=[pl.BlockSpec((tm, y.shape[1]), lambda i: (i, 0)),
                  _layer_spec(w, layer),
                  pl.BlockSpec((tm, dm), lambda i: (i, 0)),
                  _mod_spec(layer, cond, tm, dm)],
        out_specs=pl.BlockSpec((tm, dm), lambda i: (i, 0)),
        out_shape=jax.ShapeDtypeStruct((t, dm), F32),
        compiler_params=_cparams("parallel"),
        name="out_proj",
    )(y, w, x, mod)


def _ffn_kernel(x_ref, xt_ref, xb_ref, g_ref, mod_ref, wa_ref, wg_ref, cwa_ref, cwg_ref, wd_ref, *rest, seq, final):
    if final:
        fg_ref, o_ref, h_ref = rest
    else:
        o_ref, h_ref = rest
    i, k = pl.program_id(0), pl.program_id(1)
    tm = x_ref.shape[0]
    half = HALO // 2
    multi = tm > seq

    def normmod(x):
        y = _rms(x, g_ref[0])
        return (y * (1.0 + mod_ref[0, 0, _SC2:_SC2 + 1, :]) + mod_ref[0, 0, _SH2:_SH2 + 1, :]).astype(BF16)

    @pl.when(k == 0)
    def _():
        h_ref[HALO:HALO + tm, :] = normmod(x_ref[...])
        nb = normmod(jnp.concatenate([xb_ref[...], xt_ref[...]], axis=0))
        if not multi:
            from_next = lax.broadcasted_iota(jnp.int32, (HALO, 1), 0) < half
            boundary = jnp.where(from_next, ((i + 1) * tm) % seq, (i * tm) % seq)
            nb = jnp.where(boundary == 0, jnp.zeros_like(nb), nb)
        h_ref[0:HALO, :] = nb
        o_ref[...] = x_ref[...]

    if multi:
        pos = lax.broadcasted_iota(jnp.int32, (tm, 1), 0) % seq
        first, last = pos == 0, pos == seq - 1

    def conv(u, cw_ref):
        cur = u[HALO:HALO + tm]
        prev = pltpu.roll(u, 1, 0)[HALO:HALO + tm]
        nxt = pltpu.roll(u, tm + HALO - 1, 0)[HALO:HALO + tm]
        if multi:
            prev = jnp.where(first, 0.0, prev)
            nxt = jnp.where(last, 0.0, nxt)
        return cw_ref[0, 0:1, :] * prev + cw_ref[0, 1:2, :] * cur + cw_ref[0, 2:3, :] * nxt

    h = h_ref[...]
    a = conv(_dot(h, wa_ref[0]), cwa_ref)
    g = conv(_dot(h, wg_ref[0]), cwg_ref)
    act = (g * _sigmoid(g) * a).astype(BF16)
    o_ref[...] += mod_ref[0, 0, _GT2:_GT2 + 1, :] * _dot(act, wd_ref[0])

    if final:
        @pl.when(k == pl.num_programs(1) - 1)
        def _():
            o_ref[...] = _rms(o_ref[...], fg_ref[...])


def _ffn(x, gain, mod, cond, w_up, cw, w_down, layer, final_gain, *, seq, tm, tf):
    t, dm = x.shape
    fp = w_down.shape[1]
    half = HALO // 2
    assert t % tm == 0 and fp % tf == 0 and tm % HALO == 0 and t % seq == 0
    assert seq % tm == 0 or tm % seq == 0
    nk = fp // tf
    hb = tm // half
    last_hb = t // half - 1
    final = final_gain is not None
    in_specs = [pl.BlockSpec((tm, dm), lambda i, k: (i, 0)),
                pl.BlockSpec((half, dm), lambda i, k: (jnp.maximum(i * hb - 1, 0), 0)),
                pl.BlockSpec((half, dm), lambda i, k: (jnp.minimum((i + 1) * hb, last_hb), 0)),
                _layer_spec(gain, layer),
                _mod_spec(layer, cond, tm, dm),
                pl.BlockSpec((1, dm, tf), lambda i, k: (layer, 0, k)),
                pl.BlockSpec((1, dm, tf), lambda i, k: (layer, 0, nk + k)),
                pl.BlockSpec((1, 8, tf), lambda i, k: (layer, 0, k)),
                pl.BlockSpec((1, 8, tf), lambda i, k: (layer, 0, nk + k)),
                pl.BlockSpec((1, tf, dm), lambda i, k: (layer, k, 0))]
    args = [x, x, x, gain, mod, w_up, w_up, cw, cw, w_down]
    if final:
        in_specs.append(pl.BlockSpec((1, dm), lambda i, k: (0, 0)))
        args.append(final_gain)
    return pl.pallas_call(
        functools.partial(_ffn_kernel, seq=seq, final=final),
        grid=(t // tm, nk),
        in_specs=in_specs,
        out_specs=pl.BlockSpec((tm, dm), lambda i, k: (i, 0)),
        out_shape=jax.ShapeDtypeStruct((t, dm), F32),
        scratch_shapes=[pltpu.VMEM((tm + HALO, dm), BF16)],
        compiler_params=_cparams("parallel", "arbitrary"),
        name="conv_ffn",
    )(*args)


def _repack_kernel(x_ref, *o_refs, plans):
    rows = x_ref.shape[1]
    width = x_ref.shape[2]
    for o_ref, plan in zip(o_refs, plans):
        parts = []
        for seg in plan:
            if seg[0] == "zero":
                parts.append(jnp.zeros((rows, seg[1]), BF16))
            else:
                a, b = seg
                lo, hi = a // LANES * LANES, min(-(-b // LANES) * LANES, width)
                parts.append(x_ref[0, :, lo:hi][:, a - lo:b - lo].astype(BF16))
        o_ref[0] = parts[0] if len(parts) == 1 else jnp.concatenate(parts, axis=1)


def _repack(w, plans, *, tr, name):
    depth, rows, width = w.shape
    assert rows % tr == 0
    widths = [sum(s[1] if s[0] == "zero" else s[1] - s[0] for s in plan) for plan in plans]
    return pl.pallas_call(
        functools.partial(_repack_kernel, plans=plans),
        grid=(depth, rows // tr),
        in_specs=[pl.BlockSpec((1, tr, width), lambda l, r: (l, r, 0))],
        out_specs=[pl.BlockSpec((1, tr, n), lambda l, r: (l, r, 0)) for n in widths],
        out_shape=[jax.ShapeDtypeStruct((depth, rows, n), BF16) for n in widths],
        compiler_params=_cparams("parallel", "parallel"),
        name=name,
    )(w)


def _rope_tables(seq):
    t = jnp.arange(seq)
    row = (t // GRID_W).astype(F32)
    col = (t % GRID_W).astype(F32)
    half = ROPE_DIM // 2
    inv_freq = ROPE_THETA ** (-jnp.arange(0, half, 2, dtype=F32) / half)
    d = jnp.arange(ROPE_DIM)
    ang = jnp.where((d < half)[None, :], row[:, None], col[:, None]) * inv_freq[d % (half // 2)][None, :]
    upper = ((d % half) < half // 2)[None, :]
    cos = jnp.cos(ang)
    sin = jnp.sin(ang)
    sa = jnp.where(upper, -sin, 0.0)
    sb = jnp.where(upper, 0.0, sin)
    rep = lambda a: jnp.tile(a, (1, LANES // ROPE_DIM)).astype(F32)
    return rep(cos), rep(sa), rep(sb)


def _identity_tables(rows):
    return jnp.ones((rows, LANES), F32), jnp.zeros((rows, LANES), F32), jnp.zeros((rows, LANES), F32)


def _group_major(a, axis):
    order = [kv * GQA_GROUP + g for g in range(GQA_GROUP) for kv in range(GQA_KV_HEADS)]
    take = lambda h: lax.slice_in_dim(a, h * GQA_HD, (h + 1) * GQA_HD, axis=axis)
    return jnp.concatenate([take(h) for h in order], axis=axis)


def _stacked_weights(w_in, mla_w_uq, mla_w_ukv, w_branch, w_out, ffn_w_up, ffn_conv_w, ffn_w_down, pool_w):
    zeros = lambda arr, n: jnp.zeros(arr.shape[:-1] + (n,), BF16)
    gq0 = 1344
    gq_heads = [(gq0 + (kv * GQA_GROUP + g) * GQA_HD, gq0 + (kv * GQA_GROUP + g + 1) * GQA_HD)
                for g in range(GQA_GROUP) for kv in range(GQA_KV_HEADS)]
    small_plan = [(1856, 2880), (2880, 3392), (0, 512), (832, 1344), *gq_heads, (576, 704), (704, 832), (512, 576),
                  ("zero", ZS_COLS - ZS_KR - ROPE_DIM)]
    gate0 = 3392
    w_small, w_gate = _repack(w_in, [small_plan, [(gate0, gate0 + N_BRANCH * D_MODEL)]], tr=128, name="repack_w_in")
    uq, ukv = mla_w_uq.astype(BF16), mla_w_ukv.astype(BF16)
    wq, wk, wv = [], [], []
    for h in range(MLA_HEADS):
        wq += [uq[:, :, h * MLA_QK:(h + 1) * MLA_QK], zeros(uq, MLA_SLOT - MLA_QK)]
        lo = h * (MLA_NOPE + MLA_V)
        wk += [ukv[:, :, lo:lo + MLA_NOPE], zeros(ukv, MLA_SLOT - MLA_NOPE)]
        wv.append(ukv[:, :, lo + MLA_NOPE:lo + MLA_NOPE + MLA_V])
    wq, wk, wv = (jnp.concatenate(parts, axis=2) for parts in (wq, wk, wv))
    wb = w_branch.astype(BF16)
    wb = jnp.concatenate([wb[:, :2], _group_major(wb[:, 2], 1)[:, None], wb[:, 3:]], axis=1)
    padc = FF_PAD - D_FF
    (w_up,) = _repack(ffn_w_up, [[(0, D_FF), ("zero", padc), (D_FF, 2 * D_FF), ("zero", padc)]], tr=128,
                      name="repack_w_up")
    cwz = jnp.zeros(ffn_conv_w.shape[:-1] + (padc,), ffn_conv_w.dtype)
    cw = jnp.concatenate([ffn_conv_w[:, :, :D_FF], cwz, ffn_conv_w[:, :, D_FF:], cwz], axis=2)
    cw = jnp.pad(cw, ((0, 0), (0, 8 - cw.shape[1]), (0, 0)))
    w_down = jnp.concatenate([ffn_w_down.astype(BF16), jnp.zeros((ffn_w_down.shape[0], padc, D_MODEL), BF16)], axis=1)
    return dict(w_small=w_small, w_gate=w_gate, wq=wq, wk=wk, wv=wv, w_branch=wb, w_out=w_out.astype(BF16),
                w_up=w_up, ffn_cw=cw, w_down=w_down, pool_w=pool_w.astype(BF16))


def _rope_placement():
    d = jnp.arange(LANES)[:, None]
    col = jnp.arange(MLA_HEADS * MLA_SLOT)[None, :]
    return ((col % MLA_SLOT == MLA_NOPE + d) & (d < ROPE_DIM)).astype(BF16)


def _sink_rows(gqa_sink):
    depth = gqa_sink.shape[0]
    s = gqa_sink.reshape(depth, GQA_KV_HEADS, GQA_GROUP).transpose(0, 2, 1)
    s = jnp.pad(s, ((0, 0), (0, 0), (0, LANES - GQA_KV_HEADS)))
    return jnp.repeat(s, GQA_TOK, axis=1)


def _stream_front(x, mod, cond, sw, p, tabs, tab_period, layer, *, tm, gates):
    if gates:
        zs, sg = _in_proj(x, p["norm1_g"], mod, cond, sw["w_small"], sw["w_gate"], layer, tm=tm,
                          tn_small=ZS_TN, tn_gate=1024)
    else:
        zs, sg = _nm_matmul(x, p["norm1_g"], mod, cond, sw["w_small"], layer, tm=tm, tn=ZS_TN,
                            name="in_proj_small"), None
    q, k, v, gq, gk = _prep(zs, tabs, tab_period, p["mla_kv_norm"], p["mla_q_norm"], sw["wk"], sw["wv"], sw["wq"],
                            p["rope_place"], layer, tm=min(tm, 512))
    return zs, sg, q, k, v, gq, gk


def _stream_back(x, mod, cond, sw, p, zs, sg, o_mla, o_gqa, layer, final_gain, *, tm, seq):
    o_conv, o_pool = _mixers(zs, p["conv_w"], p["conv_b"], p["conv_ln_g"], p["conv_ln_b"], sw["pool_w"],
                             p["pool_scale"], layer, seq=seq)
    y = _merge((o_conv, o_mla, o_gqa, o_pool), sg, sw["w_branch"], layer, tm=min(tm, 256))
    x = _resid_matmul(y, sw["w_out"], x, mod, cond, layer, gt_row=_GT1, tm=min(tm, 512))
    return _ffn(x, p["norm2_g"], mod, cond, sw["w_up"], sw["ffn_cw"], sw["w_down"], layer, final_gain,
                seq=seq, tm=min(tm, 512), tf=512)


def kernel(x, c, ctx, c_ctx, norm1_g, norm2_g, w_ada, b_ada, w_in, mla_q_norm, mla_w_uq, mla_kv_norm, mla_w_ukv,
           gqa_sink, conv_w, conv_b, conv_ln_g, conv_ln_b, pool_w, pool_scale, w_branch, w_out, ffn_w_up,
           ffn_conv_w, ffn_w_down, final_norm_g):
    nb, seq, dm = x.shape
    lc = ctx.shape[1]
    tx, tc = nb * seq, nb * lc
    tm_x = min(1024, seq)
    tm_c = min(1024, tc)
    tq_x = min(256, seq - 2 * WINDOW)
    tq_c = min(256, lc)

    cs = jnp.zeros((16, dm), F32).at[:nb].set(c).at[nb].set(c_ctx)
    mod = _ada_mod(cs, w_ada, b_ada).reshape(DEPTH, 16, N_MOD, dm)
    mod = jnp.pad(mod, ((0, 0), (0, 0), (0, 8 - N_MOD), (0, 0)))

    sw = _stacked_weights(w_in, mla_w_uq, mla_w_ukv, w_branch, w_out, ffn_w_up, ffn_conv_w, ffn_w_down, pool_w)
    vec = lambda a: a[:, None, :]
    p = dict(norm1_g=vec(norm1_g), norm2_g=vec(norm2_g), mla_kv_norm=vec(mla_kv_norm), mla_q_norm=vec(mla_q_norm),
             rope_place=_rope_placement(), conv_w=conv_w, conv_b=vec(conv_b), conv_ln_g=vec(conv_ln_g),
             conv_ln_b=vec(conv_ln_b), pool_scale=vec(pool_scale))
    tabs_x = _rope_tables(seq)
    tabs_c = _identity_tables(min(tm_c, 512))
    sink = _sink_rows(gqa_sink)

    xs = x.reshape(tx, dm)
    cx = ctx.reshape(tc, dm)
    cond_x = (seq, 0)
    cond_c = (tc, nb)
    for l in range(DEPTH):
        last = l == DEPTH - 1
        zs_c, sg_c, q_c, k_c, v_c, gq_c, gk_c = _stream_front(cx, mod, cond_c, sw, p, tabs_c, 1, l, tm=tm_c,
                                                              gates=not last)
        zs_x, sg_x, q_x, k_x, v_x, gq_x, gk_x = _stream_front(xs, mod, cond_x, sw, p, tabs_x,
                                                              seq // min(tm_x, 512), l, tm=tm_x, gates=True)
        o_mla = _mla_attn(q_x, k_c, v_c, k_x, v_x, nb=nb, tq=min(512, seq))
        o_gqa = _gqa_attn(gq_x, gk_c, zs_c, gk_x, zs_x, sink, l, nb=nb, tq=tq_x)
        xs = _stream_back(xs, mod, cond_x, sw, p, zs_x, sg_x, o_mla, o_gqa, l,
                          final_norm_g[None] if last else None, tm=tm_x, seq=seq)
        if not last:
            o_mla_c = _mla_attn(q_c, k_c, v_c, None, None, nb=nb, tq=min(256, lc))
            o_gqa_c = _gqa_attn(gq_c, gk_c, zs_c, None, None, sink, l, nb=nb, tq=tq_c)
            cx = _stream_back(cx, mod, cond_c, sw, p, zs_c, sg_c, o_mla_c, o_gqa_c, l, None, tm=tm_c, seq=lc)

    return xs.reshape(nb, seq, dm)
```

```python
import functools

import jax
import jax.numpy as jnp
from jax import lax
from jax.experimental import pallas as pl
from jax.experimental.pallas import tpu as pltpu

F32 = jnp.float32
BF16 = jnp.bfloat16

D_MODEL = 2048
DEPTH = 4
GRID_W = 64
EPS = 1e-6
ROPE_THETA = 10000.0
ROPE_DIM = 64
MLA_HEADS = 4
MLA_NOPE = 128
MLA_V = 128
MLA_RANK = 512
MLA_QK = MLA_NOPE + ROPE_DIM
GQA_HEADS = 8
GQA_KV_HEADS = 2
GQA_GROUP = GQA_HEADS // GQA_KV_HEADS
GQA_HD = 64
WINDOW = 128
CONV_CH = 512
CONV_K = 31
POOL_WINDOWS = (2, 4, 8, 16)
POOL_GC = 128
POOL_CH = 512
N_BRANCH = 4
BRANCH_W = 512
D_FF = 5504
N_MOD = 6

LANES = 128
VMEM_LIMIT_BYTES = 56 * 1024 * 1024

ZS_CONV = 0
ZS_POOL = 1024
ZS_CKV = 1536
ZS_CQ = 2048
ZS_GQ = 2560
ZS_GK = 3072
ZS_GV = 3200
ZS_KR = 3328
ZS_COLS = 3584
ZS_TN = 512
MLA_SLOT = 256
FF_PAD = 5632
HALO = 16
CONV_PAD = 16
POOL_REACH = 8
MIX_ROWS = 64
GQA_TOK = 64
GQA_ROWS = GQA_TOK * GQA_GROUP
GQA_UNROLL = 4

_SH1, _SC1, _GT1, _SH2, _SC2, _GT2 = range(N_MOD)


def _cparams(*sem):
    return pltpu.CompilerParams(dimension_semantics=sem, vmem_limit_bytes=VMEM_LIMIT_BYTES)


def _dot(a, b):
    return jnp.dot(a, b, preferred_element_type=F32)


def _dot_nt(a, b):
    return lax.dot_general(a, b, (((1,), (1,)), ((), ())), preferred_element_type=F32)


def _sigmoid(v):
    return 1.0 / (1.0 + jnp.exp(-v))


def _rms(v, g):
    return v * lax.rsqrt(jnp.mean(v * v, axis=-1, keepdims=True) + EPS) * g


def _layer_spec(arr, layer):
    zeros = (0,) * (arr.ndim - 1)
    return pl.BlockSpec((1,) + arr.shape[1:], lambda *_: (layer,) + zeros)


def _mod_spec(layer, cond, tm, dm):
    per, base = cond
    return pl.BlockSpec((1, 1, 8, dm), lambda i, *_: (layer, base + (i * tm) // per, 0, 0))


def _ada_kernel(c_ref, w_ref, b_ref, o_ref):
    c = c_ref[...]
    s = (c * _sigmoid(c)).astype(BF16)
    o_ref[0] = _dot(s, w_ref[0].astype(BF16)) + b_ref[0]


def _ada_mod(cs, w_ada, b_ada):
    depth, dm, n = w_ada.shape
    tn = 1024
    return pl.pallas_call(
        _ada_kernel,
        grid=(depth, n // tn),
        in_specs=[pl.BlockSpec((cs.shape[0], dm), lambda l, j: (0, 0)),
                  pl.BlockSpec((1, dm, tn), lambda l, j: (l, 0, j)),
                  pl.BlockSpec((1, 1, tn), lambda l, j: (l, 0, j))],
        out_specs=pl.BlockSpec((1, cs.shape[0], tn), lambda l, j: (l, 0, j)),
        out_shape=jax.ShapeDtypeStruct((depth, cs.shape[0], n), F32),
        compiler_params=_cparams("parallel", "parallel"),
        name="ada_mod",
    )(cs, w_ada, b_ada.reshape(depth, 1, n))


def _norm1(x_ref, g_ref, mod_ref):
    y = _rms(x_ref[...], g_ref[0])
    return (y * (1.0 + mod_ref[0, 0, _SC1:_SC1 + 1, :]) + mod_ref[0, 0, _SH1:_SH1 + 1, :]).astype(BF16)


def _nm_matmul_kernel(x_ref, g_ref, mod_ref, w_ref, o_ref, h_ref):
    @pl.when(pl.program_id(1) == 0)
    def _():
        h_ref[...] = _norm1(x_ref, g_ref, mod_ref)

    o_ref[...] = _dot(h_ref[...], w_ref[0]).astype(o_ref.dtype)


def _nm_matmul(x, gain, mod, cond, w, layer, *, tm, tn, name):
    t, dm = x.shape
    n = w.shape[2]
    assert t % tm == 0 and n % tn == 0
    return pl.pallas_call(
        _nm_matmul_kernel,
        grid=(t // tm, n // tn),
        in_specs=[pl.BlockSpec((tm, dm), lambda i, j: (i, 0)),
                  _layer_spec(gain, layer),
                  _mod_spec(layer, cond, tm, dm),
                  pl.BlockSpec((1, dm, tn), lambda i, j: (layer, 0, j))],
        out_specs=pl.BlockSpec((tm, tn), lambda i, j: (i, j)),
        out_shape=jax.ShapeDtypeStruct((t, n), BF16),
        scratch_shapes=[pltpu.VMEM((tm, dm), BF16)],
        compiler_params=_cparams("parallel", "arbitrary"),
        name=name,
    )(x, gain, mod, w)


def _in_proj_kernel(x_ref, g_ref, mod_ref, ws_ref, wg_ref, zs_ref, sg_ref, h_ref, *, ns):
    j = pl.program_id(1)

    @pl.when(j == 0)
    def _():
        h_ref[...] = _norm1(x_ref, g_ref, mod_ref)

    @pl.when(j < ns)
    def _():
        zs_ref[...] = _dot(h_ref[...], ws_ref[0]).astype(BF16)

    @pl.when(j >= ns)
    def _():
        sg_ref[...] = _sigmoid(_dot(h_ref[...], wg_ref[0])).astype(BF16)


def _in_proj(x, gain, mod, cond, w_small, w_gate, layer, *, tm, tn_small, tn_gate):
    t, dm = x.shape
    n_small, n_gate = w_small.shape[2], w_gate.shape[2]
    assert t % tm == 0 and n_small % tn_small == 0 and n_gate % tn_gate == 0
    ns, ng = n_small // tn_small, n_gate // tn_gate
    small_blk = lambda j: jnp.minimum(j, ns - 1)
    gate_blk = lambda j: jnp.maximum(j - ns, 0)
    return pl.pallas_call(
        functools.partial(_in_proj_kernel, ns=ns),
        grid=(t // tm, ns + ng),
        in_specs=[pl.BlockSpec((tm, dm), lambda i, j: (i, 0)),
                  _layer_spec(gain, layer),
                  _mod_spec(layer, cond, tm, dm),
                  pl.BlockSpec((1, dm, tn_small), lambda i, j: (layer, 0, small_blk(j))),
                  pl.BlockSpec((1, dm, tn_gate), lambda i, j: (layer, 0, gate_blk(j)))],
        out_specs=[pl.BlockSpec((tm, tn_small), lambda i, j: (i, small_blk(j))),
                   pl.BlockSpec((tm, tn_gate), lambda i, j: (i, gate_blk(j)))],
        out_shape=[jax.ShapeDtypeStruct((t, n_small), BF16), jax.ShapeDtypeStruct((t, n_gate), BF16)],
        scratch_shapes=[pltpu.VMEM((tm, dm), BF16)],
        compiler_params=_cparams("parallel", "arbitrary"),
        name="in_proj",
    )(x, gain, mod, w_small, w_gate)


def _rope(v, cos, sa, sb):
    return v * cos + pltpu.roll(v, LANES - 16, 1) * sa + pltpu.roll(v, 16, 1) * sb


def _prep_kernel(ckv_ref, cq_ref, gq_ref, tail_ref, cos_ref, sa_ref, sb_ref, kvn_ref, qn_ref,
                 wk_ref, wv_ref, wq_ref, e_ref, q_out, k_out, v_out, gq_out, gk_out):
    cos, sa, sb = cos_ref[...], sa_ref[...], sb_ref[...]
    nkv = _rms(ckv_ref[...].astype(F32), kvn_ref[0]).astype(BF16)
    kr = _rope(tail_ref[:, 2 * LANES:3 * LANES].astype(F32), cos, sa, sb).astype(BF16)
    k_out[...] = (_dot(nkv, wk_ref[0]) + _dot(kr, e_ref[...])).astype(BF16)
    v_out[...] = _dot(nkv, wv_ref[0]).astype(BF16)

    nq = _rms(cq_ref[...].astype(F32), qn_ref[0]).astype(BF16)
    q = _dot(nq, wq_ref[0])
    scale = MLA_QK ** -0.5
    for h in range(MLA_HEADS):
        lo = h * MLA_SLOT
        q_out[:, lo:lo + LANES] = (q[:, lo:lo + LANES] * scale).astype(BF16)
        q_out[:, lo + LANES:lo + 2 * LANES] = (_rope(q[:, lo + LANES:lo + 2 * LANES], cos, sa, sb) * scale).astype(BF16)

    for c in range(GQA_HEADS * GQA_HD // LANES):
        sl = slice(c * LANES, (c + 1) * LANES)
        gq_out[:, sl] = (_rope(gq_ref[:, sl].astype(F32), cos, sa, sb) * GQA_HD ** -0.5).astype(BF16)
    gk_out[...] = _rope(tail_ref[:, 0:LANES].astype(F32), cos, sa, sb).astype(BF16)


def _prep(zs, tabs, tab_period, kvn, qn, wk, wv, wq, e, layer, *, tm):
    t = zs.shape[0]
    assert t % tm == 0
    cb = lambda off, w: pl.BlockSpec((tm, w), lambda i: (i, off // w))
    tab = pl.BlockSpec((tm, LANES), lambda i: (i % tab_period, 0))
    lay = lambda a: _layer_spec(a, layer)
    row = lambda w: pl.BlockSpec((tm, w), lambda i: (i, 0))
    widths = (MLA_HEADS * MLA_SLOT, MLA_HEADS * MLA_SLOT, MLA_HEADS * MLA_V, GQA_HEADS * GQA_HD, GQA_KV_HEADS * GQA_HD)
    return pl.pallas_call(
        _prep_kernel,
        grid=(t // tm,),
        in_specs=[cb(ZS_CKV, MLA_RANK), cb(ZS_CQ, MLA_RANK), cb(ZS_GQ, GQA_HEADS * GQA_HD), cb(ZS_GK, 3 * LANES),
                  tab, tab, tab, lay(kvn), lay(qn), lay(wk), lay(wv), lay(wq),
                  pl.BlockSpec(e.shape, lambda i: (0, 0))],
        out_specs=[row(w) for w in widths],
        out_shape=[jax.ShapeDtypeStruct((t, w), BF16) for w in widths],
        compiler_params=_cparams("parallel"),
        name="qkv_prep",
    )(zs, zs, zs, zs, *tabs, kvn, qn, wk, wv, wq, e)


def _mla_kernel(*refs, has_x):
    if has_x:
        q_ref, kc_ref, vc_ref, kx_ref, vx_ref, o_ref = refs
    else:
        q_ref, kc_ref, vc_ref, o_ref = refs
    for h in range(MLA_HEADS):
        ks = slice(h * MLA_SLOT, (h + 1) * MLA_SLOT)
        vs = slice(h * MLA_V, (h + 1) * MLA_V)
        qh = q_ref[:, ks]
        sc = _dot_nt(qh, kc_ref[:, ks])
        m = jnp.max(sc, axis=-1, keepdims=True)
        if has_x:
            sx = _dot_nt(qh, kx_ref[:, ks])
            m = jnp.maximum(m, jnp.max(sx, axis=-1, keepdims=True))
        pc = jnp.exp(sc - m)
        den = jnp.sum(pc, axis=-1, keepdims=True)
        o = _dot(pc.astype(BF16), vc_ref[:, vs])
        if has_x:
            px = jnp.exp(sx - m)
            den = den + jnp.sum(px, axis=-1, keepdims=True)
            o = o + _dot(px.astype(BF16), vx_ref[:, vs])
        o_ref[:, vs] = (o / den).astype(o_ref.dtype)


def _mla_attn(q, kc, vc, kx, vx, *, nb, tq):
    has_x = kx is not None
    lq = q.shape[0] // nb
    lc = kc.shape[0] // nb
    nq = lq // tq
    kw, vw = MLA_HEADS * MLA_SLOT, MLA_HEADS * MLA_V
    in_specs = [pl.BlockSpec((tq, kw), lambda b, i: (b * nq + i, 0)),
                pl.BlockSpec((lc, kw), lambda b, i: (b, 0)),
                pl.BlockSpec((lc, vw), lambda b, i: (b, 0))]
    args = [q, kc, vc]
    if has_x:
        in_specs += [pl.BlockSpec((lq, kw), lambda b, i: (b, 0)), pl.BlockSpec((lq, vw), lambda b, i: (b, 0))]
        args += [kx, vx]
    return pl.pallas_call(
        functools.partial(_mla_kernel, has_x=has_x),
        grid=(nb, nq),
        in_specs=in_specs,
        out_specs=pl.BlockSpec((tq, vw), lambda b, i: (b * nq + i, 0)),
        out_shape=jax.ShapeDtypeStruct((q.shape[0], vw), BF16),
        compiler_params=_cparams("parallel", "parallel"),
        name="mla_attn_x" if has_x else "mla_attn_c",
    )(*args)


def _gqa_kernel(*refs, has_x, lx, tq):
    if has_x:
        q_ref, kc_ref, vc_ref, kx_ref, vx_ref, sink_ref, o_ref = refs
    else:
        q_ref, kc_ref, vc_ref, sink_ref, o_ref = refs
    lower = lax.broadcasted_iota(jnp.int32, (1, LANES), 1) < GQA_HD
    slab = 3 * WINDOW

    def chunk(c, carry):
        ts = pl.multiple_of(c * GQA_TOK, GQA_TOK)
        q2 = jnp.concatenate([q_ref[pl.ds(ts, GQA_TOK), g * LANES:(g + 1) * LANES] for g in range(GQA_GROUP)], axis=0)
        zero = jnp.zeros_like(q2)
        if has_x:
            t0 = pl.program_id(1) * tq + ts
            start = pl.multiple_of(jnp.clip((t0 - WINDOW) // WINDOW * WINDOW, 0, lx - slab), LANES)
            kslab = kx_ref[pl.ds(start, slab), :]
            vslab = vx_ref[pl.ds(start, slab), :]
            qpos = t0 + lax.broadcasted_iota(jnp.int32, (GQA_ROWS, slab), 0) % GQA_TOK
            kpos = start + lax.broadcasted_iota(jnp.int32, (GQA_ROWS, slab), 1)
            valid = jnp.abs(qpos - kpos) <= WINDOW
        outs = []
        for half in range(GQA_KV_HEADS):
            qh = jnp.where(lower, q2, zero) if half == 0 else jnp.where(lower, zero, q2)
            sink = sink_ref[0, :, half:half + 1]
            sc = _dot_nt(qh, kc_ref[...])
            m = jnp.maximum(jnp.max(sc, axis=-1, keepdims=True), sink)
            if has_x:
                sw = jnp.where(valid, _dot_nt(qh, kslab), -1e30)
                m = jnp.maximum(m, jnp.max(sw, axis=-1, keepdims=True))
            pc = jnp.exp(sc - m)
            den = jnp.sum(pc, axis=-1, keepdims=True) + jnp.exp(sink - m)
            o = _dot(pc.astype(BF16), vc_ref[...])
            if has_x:
                pw = jnp.exp(sw - m)
                den = den + jnp.sum(pw, axis=-1, keepdims=True)
                o = o + _dot(pw.astype(BF16), vslab)
            outs.append(o / den)
        o2 = jnp.where(lower, outs[0], outs[1]).astype(o_ref.dtype)
        for g in range(GQA_GROUP):
            o_ref[pl.ds(ts, GQA_TOK), g * LANES:(g + 1) * LANES] = o2[g * GQA_TOK:(g + 1) * GQA_TOK]
        return carry

    lax.fori_loop(0, tq // GQA_TOK, chunk, 0, unroll=min(GQA_UNROLL, tq // GQA_TOK))


def _gqa_attn(gq, gkc, zsc, gkx, zsx, sink_rows, layer, *, nb, tq):
    has_x = gkx is not None
    lq = gq.shape[0] // nb
    lc = gkc.shape[0] // nb
    nq = lq // tq
    qw, kw = GQA_HEADS * GQA_HD, GQA_KV_HEADS * GQA_HD
    vcol = ZS_GV // kw
    assert tq % GQA_TOK == 0
    in_specs = [pl.BlockSpec((tq, qw), lambda b, i: (b * nq + i, 0)),
                pl.BlockSpec((lc, kw), lambda b, i: (b, 0)),
                pl.BlockSpec((lc, kw), lambda b, i: (b, vcol))]
    args = [gq, gkc, zsc]
    if has_x:
        assert lq >= 3 * WINDOW and WINDOW % GQA_TOK == 0
        in_specs += [pl.BlockSpec((lq, kw), lambda b, i: (b, 0)), pl.BlockSpec((lq, kw), lambda b, i: (b, vcol))]
        args += [gkx, zsx]
    in_specs.append(_layer_spec(sink_rows, layer))
    args.append(sink_rows)
    return pl.pallas_call(
        functools.partial(_gqa_kernel, has_x=has_x, lx=lq, tq=tq),
        grid=(nb, nq),
        in_specs=in_specs,
        out_specs=pl.BlockSpec((tq, qw), lambda b, i: (b * nq + i, 0)),
        out_shape=jax.ShapeDtypeStruct(gq.shape, BF16),
        compiler_params=_cparams("parallel", "parallel"),
        name="gqa_attn_x" if has_x else "gqa_attn_c",
    )(*args)


def _mixers_kernel(z_ref, cw_ref, cb_ref, lg_ref, lb_ref, pw_ref, ps_ref, oc_ref, op_ref, pa_ref, pp_ref, *, seq):
    zpad = jnp.zeros((CONV_PAD, CONV_CH), F32)
    a = z_ref[:, 0:CONV_CH].astype(F32)
    g = z_ref[:, CONV_CH:2 * CONV_CH].astype(F32)
    pa_ref[0:CONV_PAD, :] = zpad
    pa_ref[CONV_PAD + seq:2 * CONV_PAD + seq, :] = zpad
    pa_ref[CONV_PAD:CONV_PAD + seq, :] = a * _sigmoid(g)
    pp_ref[0:CONV_PAD, :] = zpad
    pp_ref[CONV_PAD + seq:2 * CONV_PAD + seq, :] = zpad
    pp_ref[CONV_PAD:CONV_PAD + seq, :] = z_ref[:, 2 * CONV_CH:2 * CONV_CH + POOL_CH].astype(F32)

    def shifted(win, j):
        return win if j == 0 else pltpu.roll(win, win.shape[0] - j, 0)

    def step(c, carry):
        r0 = pl.multiple_of(c * MIX_ROWS, MIX_ROWS)
        win = pa_ref[pl.ds(r0, MIX_ROWS + 2 * CONV_PAD), :]
        acc = jnp.zeros((MIX_ROWS, CONV_CH), F32)
        for j in range(8):
            wj = shifted(win, j)
            for q in range(2 * CONV_PAD // 8):
                k = 8 * q + j - (CONV_PAD - CONV_K // 2)
                if 0 <= k < CONV_K:
                    acc = acc + cw_ref[0, k:k + 1, :] * wj[8 * q:8 * q + MIX_ROWS, :]
        u = acc + cb_ref[0]
        mu = jnp.mean(u, axis=-1, keepdims=True)
        d = u - mu
        y = d * lax.rsqrt(jnp.mean(d * d, axis=-1, keepdims=True) + EPS) * lg_ref[0] + lb_ref[0]
        oc_ref[pl.ds(r0, MIX_ROWS), :] = (y * _sigmoid(y)).astype(BF16)
        pwin = pp_ref[pl.ds(r0 + (CONV_PAD - POOL_REACH), MIX_ROWS + 2 * POOL_REACH), :]
        sums = [jnp.zeros((MIX_ROWS, POOL_GC), F32) for _ in POOL_WINDOWS]
        for j in range(8):
            wj = shifted(pwin, j)
            for q in range(2 * POOL_REACH // 8):
                dlt = 8 * q + j - POOL_REACH
                for gi, w in enumerate(POOL_WINDOWS):
                    if -(w // 2) <= dlt < w - w // 2:
                        sums[gi] = sums[gi] + wj[8 * q:8 * q + MIX_ROWS, gi * POOL_GC:(gi + 1) * POOL_GC]
        t = r0 + lax.broadcasted_iota(jnp.int32, (MIX_ROWS, 1), 0)
        for gi, w in enumerate(POOL_WINDOWS):
            ls = slice(gi * POOL_GC, (gi + 1) * POOL_GC)
            cnt = (jnp.minimum(t - w // 2 + w, seq) - jnp.maximum(t - w // 2, 0)).astype(F32)
            p = sums[gi] / cnt - pwin[POOL_REACH:POOL_REACH + MIX_ROWS, ls]
            op_ref[pl.ds(r0, MIX_ROWS), ls] = (_dot(p.astype(BF16), pw_ref[0, gi]) * ps_ref[0][:, ls]).astype(BF16)
        return carry

    lax.fori_loop(0, seq // MIX_ROWS, step, 0)


def _mixers(zs, cw, cb, lg, lb, pw, ps, layer, *, seq):
    t = zs.shape[0]
    assert t % seq == 0 and seq % MIX_ROWS == 0
    lay = lambda a: _layer_spec(a, layer)
    zw = 2 * CONV_CH + POOL_CH
    out = pl.BlockSpec((seq, CONV_CH), lambda s: (s, 0))
    return pl.pallas_call(
        functools.partial(_mixers_kernel, seq=seq),
        grid=(t // seq,),
        in_specs=[pl.BlockSpec((seq, zw), lambda s: (s, 0)), lay(cw), lay(cb), lay(lg), lay(lb), lay(pw), lay(ps)],
        out_specs=[out, out],
        out_shape=[jax.ShapeDtypeStruct((t, CONV_CH), BF16), jax.ShapeDtypeStruct((t, POOL_CH), BF16)],
        scratch_shapes=[pltpu.VMEM((seq + 2 * CONV_PAD, CONV_CH), F32), pltpu.VMEM((seq + 2 * CONV_PAD, POOL_CH), F32)],
        compiler_params=_cparams("parallel"),
        name="local_mixers",
    )(zs, cw, cb, lg, lb, pw, ps)


def _merge_kernel(oc_ref, om_ref, og_ref, op_ref, sg_ref, wb_ref, y_ref):
    acc = None
    for n, br in enumerate((oc_ref, om_ref, og_ref, op_ref)):
        term = sg_ref[:, n * D_MODEL:(n + 1) * D_MODEL].astype(F32) * _dot(br[...], wb_ref[0, n])
        acc = term if acc is None else acc + term
    y_ref[...] = acc.astype(BF16)


def _merge(branches, sg, wb, layer, *, tm):
    t = sg.shape[0]
    assert t % tm == 0
    br = pl.BlockSpec((tm, BRANCH_W), lambda i: (i, 0))
    return pl.pallas_call(
        _merge_kernel,
        grid=(t // tm,),
        in_specs=[br, br, br, br, pl.BlockSpec((tm, N_BRANCH * D_MODEL), lambda i: (i, 0)), _layer_spec(wb, layer)],
        out_specs=pl.BlockSpec((tm, D_MODEL), lambda i: (i, 0)),
        out_shape=jax.ShapeDtypeStruct((t, D_MODEL), BF16),
        compiler_params=_cparams("parallel"),
        name="gated_merge",
    )(*branches, sg, wb)


def _resid_matmul_kernel(y_ref, w_ref, x_ref, mod_ref, o_ref, *, gt_row):
    o_ref[...] = x_ref[...] + mod_ref[0, 0, gt_row:gt_row + 1, :] * _dot(y_ref[...], w_ref[0])


def _resid_matmul(y, w, x, mod, cond, layer, *, gt_row, tm):
    t, dm = x.shape
    assert t % tm == 0
    return pl.pallas_call(
        functools.partial(_resid_matmul_kernel, gt_row=gt_row),
        grid=(t // tm,),
        in_specs=[pl.BlockSpec((tm, y.shape[1]), lambda i: (i, 0)),
                  _layer_spec(w, layer),
                  pl.BlockSpec((tm, dm), lambda i: (i, 0)),
                  _mod_spec(layer, cond, tm, dm)],
        out_specs=pl.BlockSpec((tm, dm), lambda i: (i, 0)),
        out_shape=jax.ShapeDtypeStruct((t, dm), F32),
        compiler_params=_cparams("parallel"),
        name="out_proj",
    )(y, w, x, mod)


def _ffn_kernel(x_ref, xt_ref, xb_ref, g_ref, mod_ref, wa_ref, wg_ref, cwa_ref, cwg_ref, wd_ref, *rest, seq, final):
    if final:
        fg_ref, o_ref, h_ref = rest
    else:
        o_ref, h_ref = rest
    i, k = pl.program_id(0), pl.program_id(1)
    tm = x_ref.shape[0]
    half = HALO // 2
    multi = tm > seq

    def normmod(x):
        y = _rms(x, g_ref[0])
        return (y * (1.0 + mod_ref[0, 0, _SC2:_SC2 + 1, :]) + mod_ref[0, 0, _SH2:_SH2 + 1, :]).astype(BF16)

    @pl.when(k == 0)
    def _():
        h_ref[HALO:HALO + tm, :] = normmod(x_ref[...])
        nb = normmod(jnp.concatenate([xb_ref[...], xt_ref[...]], axis=0))
        if not multi:
            from_next = lax.broadcasted_iota(jnp.int32, (HALO, 1), 0) < half
            boundary = jnp.where(from_next, ((i + 1) * tm) % seq, (i * tm) % seq)
            nb = jnp.where(boundary == 0, jnp.zeros_like(nb), nb)
        h_ref[0:HALO, :] = nb
        o_ref[...] = x_ref[...]

    if multi:
        pos = lax.broadcasted_iota(jnp.int32, (tm, 1), 0) % seq
        first, last = pos == 0, pos == seq - 1

    def conv(u, cw_ref):
        cur = u[HALO:HALO + tm]
        prev = pltpu.roll(u, 1, 0)[HALO:HALO + tm]
        nxt = pltpu.roll(u, tm + HALO - 1, 0)[HALO:HALO + tm]
        if multi:
            prev = jnp.where(first, 0.0, prev)
            nxt = jnp.where(last, 0.0, nxt)
        return cw_ref[0, 0:1, :] * prev + cw_ref[0, 1:2, :] * cur + cw_ref[0, 2:3, :] * nxt

    h = h_ref[...]
    a = conv(_dot(h, wa_ref[0]), cwa_ref)
    g = conv(_dot(h, wg_ref[0]), cwg_ref)
    act = (g * _sigmoid(g) * a).astype(BF16)
    o_ref[...] += mod_ref[0, 0, _GT2:_GT2 + 1, :] * _dot(act, wd_ref[0])

    if final:
        @pl.when(k == pl.num_programs(1) - 1)
        def _():
            o_ref[...] = _rms(o_ref[...], fg_ref[...])


def _ffn(x, gain, mod, cond, w_up, cw, w_down, layer, final_gain, *, seq, tm, tf):
    t, dm = x.shape
    fp = w_down.shape[1]
    half = HALO // 2
    assert t % tm == 0 and fp % tf == 0 and tm % HALO == 0 and t % seq == 0
    assert seq % tm == 0 or tm % seq == 0
    nk = fp // tf
    hb = tm // half
    last_hb = t // half - 1
    final = final_gain is not None
    once = pl.Buffered(1)
    in_specs = [pl.BlockSpec((tm, dm), lambda i, k: (i, 0), pipeline_mode=once),
                pl.BlockSpec((half, dm), lambda i, k: (jnp.maximum(i * hb - 1, 0), 0)),
                pl.BlockSpec((half, dm), lambda i, k: (jnp.minimum((i + 1) * hb, last_hb), 0)),
                _layer_spec(gain, layer),
                _mod_spec(layer, cond, tm, dm),
                pl.BlockSpec((1, dm, tf), lambda i, k: (layer, 0, k)),
                pl.BlockSpec((1, dm, tf), lambda i, k: (layer, 0, nk + k)),
                pl.BlockSpec((1, 8, tf), lambda i, k: (layer, 0, k)),
                pl.BlockSpec((1, 8, tf), lambda i, k: (layer, 0, nk + k)),
                pl.BlockSpec((1, tf, dm), lambda i, k: (layer, k, 0))]
    args = [x, x, x, gain, mod, w_up, w_up, cw, cw, w_down]
    if final:
        in_specs.append(pl.BlockSpec((1, dm), lambda i, k: (0, 0)))
        args.append(final_gain)
    return pl.pallas_call(
        functools.partial(_ffn_kernel, seq=seq, final=final),
        grid=(t // tm, nk),
        in_specs=in_specs,
        out_specs=pl.BlockSpec((tm, dm), lambda i, k: (i, 0), pipeline_mode=once),
        out_shape=jax.ShapeDtypeStruct((t, dm), F32),
        scratch_shapes=[pltpu.VMEM((tm + HALO, dm), BF16)],
        compiler_params=_cparams("parallel", "arbitrary"),
        name="conv_ffn",
    )(*args)


def _repack_kernel(x_ref, *o_refs, plans):
    rows = x_ref.shape[1]
    width = x_ref.shape[2]
    for o_ref, plan in zip(o_refs, plans):
        parts = []
        for seg in plan:
            if seg[0] == "zero":
                parts.append(jnp.zeros((rows, seg[1]), BF16))
            else:
                a, b = seg
                lo, hi = a // LANES * LANES, min(-(-b // LANES) * LANES, width)
                parts.append(x_ref[0, :, lo:hi][:, a - lo:b - lo].astype(BF16))
        o_ref[0] = parts[0] if len(parts) == 1 else jnp.concatenate(parts, axis=1)


def _repack(w, plans, *, tr, name):
    depth, rows, width = w.shape
    assert rows % tr == 0
    widths = [sum(s[1] if s[0] == "zero" else s[1] - s[0] for s in plan) for plan in plans]
    return pl.pallas_call(
        functools.partial(_repack_kernel, plans=plans),
        grid=(depth, rows // tr),
        in_specs=[pl.BlockSpec((1, tr, width), lambda l, r: (l, r, 0))],
        out_specs=[pl.BlockSpec((1, tr, n), lambda l, r: (l, r, 0)) for n in widths],
        out_shape=[jax.ShapeDtypeStruct((depth, rows, n), BF16) for n in widths],
        compiler_params=_cparams("parallel", "parallel"),
        name=name,
    )(w)


def _rope_tables(seq):
    t = jnp.arange(seq)
    row = (t // GRID_W).astype(F32)
    col = (t % GRID_W).astype(F32)
    half = ROPE_DIM // 2
    inv_freq = ROPE_THETA ** (-jnp.arange(0, half, 2, dtype=F32) / half)
    d = jnp.arange(ROPE_DIM)
    ang = jnp.where((d < half)[None, :], row[:, None], col[:, None]) * inv_freq[d % (half // 2)][None, :]
    upper = ((d % half) < half // 2)[None, :]
    cos = jnp.cos(ang)
    sin = jnp.sin(ang)
    sa = jnp.where(upper, -sin, 0.0)
    sb = jnp.where(upper, 0.0, sin)
    rep = lambda a: jnp.tile(a, (1, LANES // ROPE_DIM)).astype(F32)
    return rep(cos), rep(sa), rep(sb)


def _identity_tables(rows):
    return jnp.ones((rows, LANES), F32), jnp.zeros((rows, LANES), F32), jnp.zeros((rows, LANES), F32)


def _group_major(a, axis):
    order = [kv * GQA_GROUP + g for g in range(GQA_GROUP) for kv in range(GQA_KV_HEADS)]
    take = lambda h: lax.slice_in_dim(a, h * GQA_HD, (h + 1) * GQA_HD, axis=axis)
    return jnp.concatenate([take(h) for h in order], axis=axis)


def _stacked_weights(w_in, mla_w_uq, mla_w_ukv, w_branch, w_out, ffn_w_up, ffn_conv_w, ffn_w_down, pool_w):
    zeros = lambda arr, n: jnp.zeros(arr.shape[:-1] + (n,), BF16)
    gq0 = 1344
    gq_heads = [(gq0 + (kv * GQA_GROUP + g) * GQA_HD, gq0 + (kv * GQA_GROUP + g + 1) * GQA_HD)
                for g in range(GQA_GROUP) for kv in range(GQA_KV_HEADS)]
    small_plan = [(1856, 2880), (2880, 3392), (0, 512), (832, 1344), *gq_heads, (576, 704), (704, 832), (512, 576),
                  ("zero", ZS_COLS - ZS_KR - ROPE_DIM)]
    gate0 = 3392
    w_small, w_gate = _repack(w_in, [small_plan, [(gate0, gate0 + N_BRANCH * D_MODEL)]], tr=128, name="repack_w_in")
    uq, ukv = mla_w_uq.astype(BF16), mla_w_ukv.astype(BF16)
    wq, wk, wv = [], [], []
    for h in range(MLA_HEADS):
        wq += [uq[:, :, h * MLA_QK:(h + 1) * MLA_QK], zeros(uq, MLA_SLOT - MLA_QK)]
        lo = h * (MLA_NOPE + MLA_V)
        wk += [ukv[:, :, lo:lo + MLA_NOPE], zeros(ukv, MLA_SLOT - MLA_NOPE)]
        wv.append(ukv[:, :, lo + MLA_NOPE:lo + MLA_NOPE + MLA_V])
    wq, wk, wv = (jnp.concatenate(parts, axis=2) for parts in (wq, wk, wv))
    wb = w_branch.astype(BF16)
    wb = jnp.concatenate([wb[:, :2], _group_major(wb[:, 2], 1)[:, None], wb[:, 3:]], axis=1)
    padc = FF_PAD - D_FF
    (w_up,) = _repack(ffn_w_up, [[(0, D_FF), ("zero", padc), (D_FF, 2 * D_FF), ("zero", padc)]], tr=128,
                      name="repack_w_up")
    cwz = jnp.zeros(ffn_conv_w.shape[:-1] + (padc,), ffn_conv_w.dtype)
    cw = jnp.concatenate([ffn_conv_w[:, :, :D_FF], cwz, ffn_conv_w[:, :, D_FF:], cwz], axis=2)
    cw = jnp.pad(cw, ((0, 0), (0, 8 - cw.shape[1]), (0, 0)))
    w_down = jnp.concatenate([ffn_w_down.astype(BF16), jnp.zeros((ffn_w_down.shape[0], padc, D_MODEL), BF16)], axis=1)
    return dict(w_small=w_small, w_gate=w_gate, wq=wq, wk=wk, wv=wv, w_branch=wb, w_out=w_out.astype(BF16),
                w_up=w_up, ffn_cw=cw, w_down=w_down, pool_w=pool_w.astype(BF16))


def _rope_placement():
    d = jnp.arange(LANES)[:, None]
    col = jnp.arange(MLA_HEADS * MLA_SLOT)[None, :]
    return ((col % MLA_SLOT == MLA_NOPE + d) & (d < ROPE_DIM)).astype(BF16)


def _sink_rows(gqa_sink):
    depth = gqa_sink.shape[0]
    s = gqa_sink.reshape(depth, GQA_KV_HEADS, GQA_GROUP).transpose(0, 2, 1)
    s = jnp.pad(s, ((0, 0), (0, 0), (0, LANES - GQA_KV_HEADS)))
    return jnp.repeat(s, GQA_TOK, axis=1)


def _stream_front(x, mod, cond, sw, p, tabs, tab_period, layer, *, tm, gates):
    if gates:
        zs, sg = _in_proj(x, p["norm1_g"], mod, cond, sw["w_small"], sw["w_gate"], layer, tm=tm,
                          tn_small=ZS_TN, tn_gate=1024)
    else:
        zs, sg = _nm_matmul(x, p["norm1_g"], mod, cond, sw["w_small"], layer, tm=tm, tn=ZS_TN,
                            name="in_proj_small"), None
    q, k, v, gq, gk = _prep(zs, tabs, tab_period, p["mla_kv_norm"], p["mla_q_norm"], sw["wk"], sw["wv"], sw["wq"],
                            p["rope_place"], layer, tm=min(tm, 512))
    return zs, sg, q, k, v, gq, gk


def _stream_back(x, mod, cond, sw, p, zs, sg, o_mla, o_gqa, layer, final_gain, *, tm, seq):
    o_conv, o_pool = _mixers(zs, p["conv_w"], p["conv_b"], p["conv_ln_g"], p["conv_ln_b"], sw["pool_w"],
                             p["pool_scale"], layer, seq=seq)
    y = _merge((o_conv, o_mla, o_gqa, o_pool), sg, sw["w_branch"], layer, tm=min(tm, 256))
    x = _resid_matmul(y, sw["w_out"], x, mod, cond, layer, gt_row=_GT1, tm=min(tm, 512))
    return _ffn(x, p["norm2_g"], mod, cond, sw["w_up"], sw["ffn_cw"], sw["w_down"], layer, final_gain,
                seq=seq, tm=tm, tf=512)


def kernel(x, c, ctx, c_ctx, norm1_g, norm2_g, w_ada, b_ada, w_in, mla_q_norm, mla_w_uq, mla_kv_norm, mla_w_ukv,
           gqa_sink, conv_w, conv_b, conv_ln_g, conv_ln_b, pool_w, pool_scale, w_branch, w_out, ffn_w_up,
           ffn_conv_w, ffn_w_down, final_norm_g):
    nb, seq, dm = x.shape
    lc = ctx.shape[1]
    tx, tc = nb * seq, nb * lc
    tm_x = min(1024, seq)
    tm_c = min(1024, tc)
    tq_x = min(256, seq - 2 * WINDOW)
    tq_c = min(256, lc)

    cs = jnp.zeros((16, dm), F32).at[:nb].set(c).at[nb].set(c_ctx)
    mod = _ada_mod(cs, w_ada, b_ada).reshape(DEPTH, 16, N_MOD, dm)
    mod = jnp.pad(mod, ((0, 0), (0, 0), (0, 8 - N_MOD), (0, 0)))

    sw = _stacked_weights(w_in, mla_w_uq, mla_w_ukv, w_branch, w_out, ffn_w_up, ffn_conv_w, ffn_w_down, pool_w)
    vec = lambda a: a[:, None, :]
    p = dict(norm1_g=vec(norm1_g), norm2_g=vec(norm2_g), mla_kv_norm=vec(mla_kv_norm), mla_q_norm=vec(mla_q_norm),
             rope_place=_rope_placement(), conv_w=conv_w, conv_b=vec(conv_b), conv_ln_g=vec(conv_ln_g),
             conv_ln_b=vec(conv_ln_b), pool_scale=vec(pool_scale))
    tabs_x = _rope_tables(seq)
    tabs_c = _identity_tables(min(tm_c, 512))
    sink = _sink_rows(gqa_sink)

    xs = x.reshape(tx, dm)
    cx = ctx.reshape(tc, dm)
    cond_x = (seq, 0)
    cond_c = (tc, nb)
    for l in range(DEPTH):
        last = l == DEPTH - 1
        zs_c, sg_c, q_c, k_c, v_c, gq_c, gk_c = _stream_front(cx, mod, cond_c, sw, p, tabs_c, 1, l, tm=tm_c,
                                                              gates=not last)
        zs_x, sg_x, q_x, k_x, v_x, gq_x, gk_x = _stream_front(xs, mod, cond_x, sw, p, tabs_x,
                                                              seq // min(tm_x, 512), l, tm=tm_x, gates=True)
        o_mla = _mla_attn(q_x, k_c, v_c, k_x, v_x, nb=nb, tq=min(512, seq))
        o_gqa = _gqa_attn(gq_x, gk_c, zs_c, gk_x, zs_x, sink, l, nb=nb, tq=tq_x)
        xs = _stream_back(xs, mod, cond_x, sw, p, zs_x, sg_x, o_mla, o_gqa, l,
                          final_norm_g[None] if last else None, tm=tm_x, seq=seq)
        if not last:
            o_mla_c = _mla_attn(q_c, k_c, v_c, None, None, nb=nb, tq=min(256, lc))
            o_gqa_c = _gqa_attn(gq_c, gk_c, zs_c, None, None, sink, l, nb=nb, tq=tq_c)
            cx = _stream_back(cx, mod, cond_c, sw, p, zs_c, sg_c, o_mla_c, o_gqa_c, l, None, tm=tm_c, seq=lc)

    return xs.reshape(nb, seq, dm)
```
